```python
import math
import jax, jax.numpy as jnp
from jax import lax
import numpy as np

D_MODEL = 4096
BATCH = 1
SEQ = 8192
DEPTH = 4

CHUNK = 64
CONV_CH = D_MODEL // 2
CONV_K = 31
SSM_HEAD_DIM = 64
D_INNER = D_MODEL // 2
SSM_HEADS = D_INNER // SSM_HEAD_DIM
SSM_GROUPS = 4
D_STATE = 128
SSM_CONV_K = 4
XBC_DIM = D_INNER + 2 * SSM_GROUPS * D_STATE
DT_MIN = 0.001
DT_MAX = 0.1
DT_PROJ_SCALE = 0.1
N_BRANCHES = 2
SPLITS = (2 * CONV_CH,
          2 * CONV_CH + D_INNER,
          2 * CONV_CH + D_INNER + XBC_DIM,
          2 * CONV_CH + D_INNER + XBC_DIM + SSM_HEADS)
IN_PROJ_DIM = SPLITS[3] + N_BRANCHES * D_MODEL
D_FF_DENSE = 3 * D_MODEL // 2
N_EXPERTS = 8
TOP_K = 2
D_FF_EXPERT = 3 * D_MODEL // 8
N_DENSE = (DEPTH + 1) // 2
N_MOE = DEPTH // 2
DEEPNORM_ALPHA = (2.0 * DEPTH) ** 0.25
DEEPNORM_BETA = (8.0 * DEPTH) ** -0.25
LN_EPS = 1e-5

kernel_name = "hybrid_conformer_ssd_moe_deepnorm"


def layer_norm(x, g, b):
    xf = x.astype(jnp.float32)
    mu = jnp.mean(xf, axis=-1, keepdims=True)
    var = jnp.mean(jnp.square(xf - mu), axis=-1, keepdims=True)
    return ((xf - mu) * lax.rsqrt(var + LN_EPS)).astype(x.dtype) * g + b


def causal_depthwise_conv(x, w):
    k, c = w.shape
    return lax.conv_general_dilated(
        x, w[:, None, :], window_strides=(1,), padding=[(k - 1, 0)],
        dimension_numbers=("NWC", "WIO", "NWC"), feature_group_count=c)


def gated_group_rmsnorm(y, z, g):
    b, L, d = y.shape
    h = (y.astype(jnp.float32) * jax.nn.silu(z.astype(jnp.float32)))
    h = h.reshape(b, L, SSM_GROUPS, d // SSM_GROUPS)
    h = h * lax.rsqrt(jnp.mean(h * h, axis=-1, keepdims=True) + LN_EPS)
    return h.reshape(b, L, d).astype(z.dtype) * g


def ssd_chunked(x, dt, A, B, C):
    b, L, H, P = x.shape
    G, N = B.shape[-2:]
    E = H // G
    nc = L // CHUNK
    xs = (x.astype(jnp.float32) * dt[..., None]).reshape(b, nc, CHUNK, G, E, P)
    dA = (dt * A).reshape(b, nc, CHUNK, G, E)
    dA = jnp.moveaxis(dA, 2, -1)
    cs = jnp.cumsum(dA, axis=-1)
    Bc = B.astype(jnp.float32).reshape(b, nc, CHUNK, G, N)
    Cc = C.astype(jnp.float32).reshape(b, nc, CHUNK, G, N)
    causal = jnp.tril(jnp.ones((CHUNK, CHUNK), dtype=bool))
    seg = cs[..., :, None] - cs[..., None, :]
    decay = jnp.exp(jnp.where(causal, seg, -jnp.inf))
    scores = jnp.einsum("bclgn,bcsgn->bcgls", Cc, Bc)
    y_diag = jnp.einsum("bcgls,bcgels,bcsgep->bclgep", scores, decay, xs)
    decay_to_end = jnp.exp(cs[..., -1:] - cs)
    chunk_states = jnp.einsum("bclgn,bcgel,bclgep->bcgepn", Bc, decay_to_end, xs)
    chunk_decay = jnp.exp(cs[..., -1])

    def step(state, inp):
        s_c, a_c = inp
        return state * a_c[..., None, None] + s_c, state

    init = jnp.zeros((b, G, E, P, N), jnp.float32)
    _, prev_states = lax.scan(step, init, (jnp.moveaxis(chunk_states, 1, 0),
                                           jnp.moveaxis(chunk_decay, 1, 0)))
    prev_states = jnp.moveaxis(prev_states, 0, 1)
    y_off = jnp.einsum("bclgn,bcgepn,bcgel->bclgep", Cc, prev_states, jnp.exp(cs))
    return (y_diag + y_off).reshape(b, L, H, P)


def hybrid_mixer(x, w_in, b_gate, conv_dw, conv_ln_g, conv_ln_b, conv_pw2,
                 ssm_conv_w, ssm_conv_b, dt_bias, a_log, d_skip, ssm_norm_g,
                 ssm_out, w_o):
    b, L, _ = x.shape
    proj = x @ w_in
    glu_in, z, xbc, dt, gates = jnp.split(proj, SPLITS, axis=-1)

    u_a, u_g = jnp.split(glu_in, 2, axis=-1)
    u = u_a * jax.nn.sigmoid(u_g)
    u = causal_depthwise_conv(u, conv_dw)
    u = jax.nn.silu(layer_norm(u, conv_ln_g, conv_ln_b))
    conv_branch = u @ conv_pw2

    xbc = jax.nn.silu(causal_depthwise_conv(xbc, ssm_conv_w) + ssm_conv_b)
    xs, Bm, Cm = jnp.split(xbc, (D_INNER, D_INNER + SSM_GROUPS * D_STATE), axis=-1)
    xs = xs.reshape(b, L, SSM_HEADS, SSM_HEAD_DIM)
    Bm = Bm.reshape(b, L, SSM_GROUPS, D_STATE)
    Cm = Cm.reshape(b, L, SSM_GROUPS, D_STATE)
    dt = jax.nn.softplus((dt + dt_bias).astype(jnp.float32))
    A = -jnp.exp(a_log.astype(jnp.float32))
    y = ssd_chunked(xs, dt, A, Bm, Cm) + d_skip[:, None].astype(jnp.float32) * xs
    y = gated_group_rmsnorm(y.reshape(b, L, D_INNER).astype(x.dtype), z, ssm_norm_g)
    ssm_branch = y @ ssm_out

    g_conv, g_ssm = jnp.split(jax.nn.sigmoid(gates + b_gate), N_BRANCHES, axis=-1)
    return (g_conv * conv_branch + g_ssm * ssm_branch) @ w_o


def swiglu(x, w_gate, w_up, w_down):
    return (jax.nn.silu(x @ w_gate) * (x @ w_up)) @ w_down


def moe_swiglu(x, w_router, w_gate, w_up, w_down):
    logits = (x @ w_router).astype(jnp.float32)
    top_vals, top_idx = lax.top_k(logits, TOP_K)
    top_w = jax.nn.softmax(top_vals, axis=-1)
    combine = jnp.sum(jax.nn.one_hot(top_idx, N_EXPERTS, dtype=jnp.float32)
                      * top_w[..., None], axis=-2).astype(x.dtype)
    h = jax.nn.silu(jnp.einsum("bld,edf->eblf", x, w_gate)) \
        * jnp.einsum("bld,edf->eblf", x, w_up)
    h = h * jnp.moveaxis(combine, -1, 0)[..., None]
    return jnp.einsum("eblf,efd->bld", h, w_down)


def setup_inputs(seed: int = 0) -> dict:
    key = jax.random.key(seed)
    ks = jax.random.split(key, 32)
    f32 = jnp.float32

    def nrm(k, shape, scale):
        return jax.random.normal(k, shape, f32) * scale

    beta = DEEPNORM_BETA
    col_scale = jnp.ones((IN_PROJ_DIM,), f32).at[SPLITS[2]:SPLITS[3]].set(DT_PROJ_SCALE)
    u = jax.random.uniform(ks[9], (DEPTH, SSM_HEADS), f32)
    dt0 = jnp.maximum(jnp.exp(u * (math.log(DT_MAX) - math.log(DT_MIN)) + math.log(DT_MIN)), 1e-4)
    return {
        "x": nrm(ks[0], (BATCH, SEQ, D_MODEL), 1.0),
        "w_in": nrm(ks[1], (DEPTH, D_MODEL, IN_PROJ_DIM), D_MODEL ** -0.5) * col_scale,
        "b_gate": nrm(ks[2], (DEPTH, N_BRANCHES * D_MODEL), 0.02),
        "conv_dw": nrm(ks[3], (DEPTH, CONV_K, CONV_CH), CONV_K ** -0.5),
        "conv_ln_g": 1.0 + nrm(ks[4], (DEPTH, CONV_CH), 0.02),
        "conv_ln_b": nrm(ks[5], (DEPTH, CONV_CH), 0.02),
        "conv_pw2": nrm(ks[6], (DEPTH, CONV_CH, D_MODEL), beta * CONV_CH ** -0.5),
        "ssm_conv_w": nrm(ks[7], (DEPTH, SSM_CONV_K, XBC_DIM), SSM_CONV_K ** -0.5),
        "ssm_conv_b": nrm(ks[8], (DEPTH, XBC_DIM), 0.02),
        "dt_bias": dt0 + jnp.log(-jnp.expm1(-dt0)),
        "a_log": jnp.log(jax.random.uniform(ks[10], (DEPTH, SSM_HEADS), f32, 1.0, 16.0)),
        "d_skip": 1.0 + nrm(ks[11], (DEPTH, SSM_HEADS), 0.02),
        "ssm_norm_g": 1.0 + nrm(ks[12], (DEPTH, D_INNER), 0.02),
        "ssm_out": nrm(ks[13], (DEPTH, D_INNER, D_MODEL), beta * D_INNER ** -0.5),
        "w_o": nrm(ks[14], (DEPTH, D_MODEL, D_MODEL), beta * D_MODEL ** -0.5),
        "ln_mix_g": 1.0 + nrm(ks[15], (DEPTH, D_MODEL), 0.02),
        "ln_mix_b": nrm(ks[16], (DEPTH, D_MODEL), 0.02),
        "ffn_w_gate": nrm(ks[17], (N_DENSE, D_MODEL, D_FF_DENSE), D_MODEL ** -0.5),
        "ffn_w_up": nrm(ks[18], (N_DENSE, D_MODEL, D_FF_DENSE), D_MODEL ** -0.5),
        "ffn_w_down": nrm(ks[19], (N_DENSE, D_FF_DENSE, D_MODEL), beta * D_FF_DENSE ** -0.5),
        "moe_router": nrm(ks[20], (N_MOE, D_MODEL, N_EXPERTS), D_MODEL ** -0.5),
        "moe_w_gate": nrm(ks[21], (N_MOE, N_EXPERTS, D_MODEL, D_FF_EXPERT), D_MODEL ** -0.5),
        "moe_w_up": nrm(ks[22], (N_MOE, N_EXPERTS, D_MODEL, D_FF_EXPERT), D_MODEL ** -0.5),
        "moe_w_down": nrm(ks[23], (N_MOE, N_EXPERTS, D_FF_EXPERT, D_MODEL), beta * D_FF_EXPERT ** -0.5),
        "ln_ffn_g": 1.0 + nrm(ks[24], (DEPTH, D_MODEL), 0.02),
        "ln_ffn_b": nrm(ks[25], (DEPTH, D_MODEL), 0.02),
    }


def reference(x, w_in, b_gate, conv_dw, conv_ln_g, conv_ln_b, conv_pw2,
              ssm_conv_w, ssm_conv_b, dt_bias, a_log, d_skip, ssm_norm_g,
              ssm_out, w_o, ln_mix_g, ln_mix_b, ffn_w_gate, ffn_w_up,
              ffn_w_down, moe_router, moe_w_gate, moe_w_up, moe_w_down,
              ln_ffn_g, ln_ffn_b):
    for layer in range(DEPTH):
        mix = hybrid_mixer(x, w_in[layer], b_gate[layer], conv_dw[layer],
                           conv_ln_g[layer], conv_ln_b[layer], conv_pw2[layer],
                           ssm_conv_w[layer], ssm_conv_b[layer], dt_bias[layer],
                           a_log[layer], d_skip[layer], ssm_norm_g[layer],
                           ssm_out[layer], w_o[layer])
        x = layer_norm(DEEPNORM_ALPHA * x + mix, ln_mix_g[layer], ln_mix_b[layer])
        i = layer // 2
        if layer % 2 == 0:
            ff = swiglu(x, ffn_w_gate[i], ffn_w_up[i], ffn_w_down[i])
        else:
            ff = moe_swiglu(x, moe_router[i], moe_w_gate[i], moe_w_up[i], moe_w_down[i])
        x = layer_norm(DEEPNORM_ALPHA * x + ff, ln_ffn_g[layer], ln_ffn_b[layer])
    return x
```

```python
import functools

import jax
import jax.numpy as jnp
from jax import lax
from jax.experimental import pallas as pl
from jax.experimental.pallas import tpu as pltpu

F32 = jnp.float32
BF16 = jnp.bfloat16

LANES = 128
SUBLANES = 8
VMEM_LIMIT_BYTES = 56 * 1024 * 1024

SSM_GROUPS = 4
SSM_HEAD_DIM = 64
D_STATE = 128
SSM_CONV_K = 4
TOP_K = 2
LN_EPS = 1e-5
CONV_HALO = 32
SSM_HALO = 8


def _params(n_axes):
    return pltpu.CompilerParams(dimension_semantics=("arbitrary",) * n_axes,
                                vmem_limit_bytes=VMEM_LIMIT_BYTES)


def _tile(dim, pref):
    if dim <= pref:
        return dim
    t = pref
    while dim % t:
        t -= LANES
    assert t > 0, (dim, pref)
    return t


def _sigmoid(v):
    return 1.0 / (1.0 + jnp.exp(-v))


def _silu(v):
    return v * _sigmoid(v)


def _dot(a, b):
    return jnp.dot(a, b, preferred_element_type=F32)


def _cast_weight_once(w_ref, wb_ref):
    @pl.when(pl.program_id(1) == 0)
    def _():
        wb_ref[...] = w_ref[...].astype(BF16)


def _mm_plain_kernel(x_ref, w_ref, o_ref, wb_ref):
    _cast_weight_once(w_ref, wb_ref)
    o_ref[...] = _dot(x_ref[...], wb_ref[...]).astype(o_ref.dtype)


def _mm_sigmoid_bias_kernel(x_ref, w_ref, b_ref, o_ref, wb_ref):
    _cast_weight_once(w_ref, wb_ref)
    o_ref[...] = _sigmoid(_dot(x_ref[...], wb_ref[...]) + b_ref[...]).astype(o_ref.dtype)


def _mm_residual_kernel(x_ref, w_ref, r_ref, o_ref, wb_ref, *, alpha):
    _cast_weight_once(w_ref, wb_ref)
    o_ref[...] = alpha * r_ref[...] + _dot(x_ref[...], wb_ref[...])


def _mm_glu_kernel(x_ref, wa_ref, wg_ref, o_ref, wab_ref, wgb_ref):
    _cast_weight_once(wa_ref, wab_ref)
    _cast_weight_once(wg_ref, wgb_ref)
    x = x_ref[...]
    o_ref[...] = (_dot(x, wab_ref[...]) * _sigmoid(_dot(x, wgb_ref[...]))).astype(o_ref.dtype)


def _mm_swiglu_kernel(x_ref, wg_ref, wu_ref, o_ref, wgb_ref, wub_ref):
    _cast_weight_once(wg_ref, wgb_ref)
    _cast_weight_once(wu_ref, wub_ref)
    x = x_ref[...]
    o_ref[...] = (_silu(_dot(x, wgb_ref[...])) * _dot(x, wub_ref[...])).astype(o_ref.dtype)


def _mm_swiglu_routed_kernel(x_ref, wg_ref, wu_ref, c_ref, o_ref, wgb_ref, wub_ref, *, tiles_per_expert):
    _cast_weight_once(wg_ref, wgb_ref)
    _cast_weight_once(wu_ref, wub_ref)
    x = x_ref[...]
    expert = pl.program_id(0) // tiles_per_expert
    comb = c_ref[...]
    lane = lax.broadcasted_iota(jnp.int32, comb.shape, 1)
    c = jnp.sum(jnp.where(lane == expert, comb, 0.0), axis=-1, keepdims=True)
    h = _silu(_dot(x, wgb_ref[...])) * _dot(x, wub_ref[...])
    o_ref[...] = (h * c).astype(o_ref.dtype)


def _mm_merge_kernel(u_ref, y_ref, w1_ref, w2_ref, gc_ref, gs_ref, o_ref, w1b_ref, w2b_ref):
    _cast_weight_once(w1_ref, w1b_ref)
    _cast_weight_once(w2_ref, w2b_ref)
    conv_branch = _dot(u_ref[...], w1b_ref[...])
    ssm_branch = _dot(y_ref[...], w2b_ref[...])
    o_ref[...] = (gc_ref[...] * conv_branch + gs_ref[...] * ssm_branch).astype(o_ref.dtype)


def _wspec(w, lead, k0, kdim, tn, col0):
    nlead = len(lead)
    assert w.ndim == nlead + 2
    return pl.BlockSpec((None,) * nlead + (kdim, tn), lambda n, m: tuple(lead) + (k0, col0 + n))


def _matmul(kernel, x_list, w_list, extra, out_cols, out_dtype, *, tm, tn, name):
    m_rows = x_list[0][0].shape[0]
    in_specs, operands, scratch = [], [], []
    for x, kb, kd in x_list:
        in_specs.append(pl.BlockSpec((tm, kd), lambda n, m, kb=kb: (m, kb)))
        operands.append(x)
    for w, lead, kb, col0, kd in w_list:
        in_specs.append(_wspec(w, lead, kb, kd, tn, col0))
        operands.append(w)
        scratch.append(pltpu.VMEM((kd, tn), BF16))
    for a, spec in extra:
        in_specs.append(spec)
        operands.append(a)
    return pl.pallas_call(
        kernel,
        out_shape=jax.ShapeDtypeStruct((m_rows, out_cols), out_dtype),
        grid=(out_cols // tn, m_rows // tm),
        in_specs=in_specs,
        out_specs=pl.BlockSpec((tm, tn), lambda n, m: (m, n)),
        scratch_shapes=scratch,
        compiler_params=_params(2),
        name=name,
    )(*operands)


def _mm_single(kernel, x, w, lead, col_start, n_cols, out_dtype, *, tm, tn, extra=(), name, kb=0, kd=None):
    kd = x.shape[1] if kd is None else kd
    assert col_start % tn == 0 and n_cols % tn == 0
    return _matmul(kernel, [(x, kb, kd)], [(w, lead, kb, col_start // tn, kd)], list(extra), n_cols, out_dtype,
                   tm=tm, tn=tn, name=name)


def _ln_kernel(h_ref, g_ref, b_ref, of_ref, ob_ref):
    h = h_ref[...]
    mu = jnp.mean(h, axis=-1, keepdims=True)
    d = h - mu
    var = jnp.mean(d * d, axis=-1, keepdims=True)
    y = d * lax.rsqrt(var + LN_EPS) * g_ref[...] + b_ref[...]
    of_ref[...] = y
    ob_ref[...] = y.astype(BF16)


def _layer_norm(h, g, b):
    rows, d = h.shape
    tr = _tile(rows, 256)
    row_spec = pl.BlockSpec((tr, d), lambda i: (i, 0))
    vec_spec = pl.BlockSpec((1, d), lambda i: (0, 0))
    return pl.pallas_call(
        _ln_kernel,
        out_shape=(jax.ShapeDtypeStruct((rows, d), F32), jax.ShapeDtypeStruct((rows, d), BF16)),
        grid=(rows // tr,),
        in_specs=[row_spec, vec_spec, vec_spec],
        out_specs=(row_spec, row_spec),
        compiler_params=_params(1),
        name="layer_norm",
    )(h, g.reshape(1, d), b.reshape(1, d))


def _conv_branch_kernel(u_ref, up_ref, w_ref, g_ref, b_ref, o_ref, xin_ref, cv_ref, *, taps, sub_rows):
    rows, ch = u_ref.shape
    first = pl.program_id(0) == 0
    xin_ref[0:CONV_HALO, :] = jnp.where(first, 0.0, up_ref[...])
    xin_ref[CONV_HALO:, :] = u_ref[...]

    def chunk(c, carry):
        lanes = pl.ds(pl.multiple_of(c * LANES, LANES), LANES)
        for r in range(rows // sub_rows):
            acc = jnp.zeros((sub_rows, LANES), F32)
            for k in range(taps):
                start = CONV_HALO - (taps - 1) + k + r * sub_rows
                acc = acc + xin_ref[pl.ds(start, sub_rows), lanes] * w_ref[pl.ds(k, 1), lanes]
            cv_ref[pl.ds(r * sub_rows, sub_rows), lanes] = acc
        return carry

    lax.fori_loop(0, ch // LANES, chunk, 0)
    v = cv_ref[...]
    mu = jnp.mean(v, axis=-1, keepdims=True)
    d = v - mu
    var = jnp.mean(d * d, axis=-1, keepdims=True)
    y = d * lax.rsqrt(var + LN_EPS) * g_ref[...] + b_ref[...]
    o_ref[...] = _silu(y).astype(o_ref.dtype)


def _conv_branch(u, conv_w, ln_g, ln_b):
    rows, ch = u.shape
    taps = conv_w.shape[0]
    assert taps - 1 <= CONV_HALO
    tr = _tile(rows, 256)
    halo_blocks = tr // CONV_HALO
    kernel = functools.partial(_conv_branch_kernel, taps=taps, sub_rows=_tile(tr, 64))
    vec_spec = pl.BlockSpec((1, ch), lambda i: (0, 0))
    return pl.pallas_call(
        kernel,
        out_shape=jax.ShapeDtypeStruct((rows, ch), BF16),
        grid=(rows // tr,),
        in_specs=[pl.BlockSpec((tr, ch), lambda i: (i, 0)),
                  pl.BlockSpec((CONV_HALO, ch), lambda i: (jnp.maximum(i * halo_blocks - 1, 0), 0)),
                  pl.BlockSpec((taps, ch), lambda i: (0, 0)),
                  vec_spec, vec_spec],
        out_specs=pl.BlockSpec((tr, ch), lambda i: (i, 0)),
        scratch_shapes=[pltpu.VMEM((tr + CONV_HALO, ch), F32), pltpu.VMEM((tr, ch), F32)],
        compiler_params=_params(1),
        name="conv_branch",
    )(u, u, conv_w, ln_g.reshape(1, ch), ln_b.reshape(1, ch))


def _split3(v):
    hi = v.astype(BF16)
    r1 = v - hi.astype(F32)
    mid = r1.astype(BF16)
    lo = (r1 - mid.astype(F32)).astype(BF16)
    return hi, mid, lo


def _dt_kernel(raw_ref, bias_ref, alog_ref, tri_ref, place_ref,
               cs_ref, cst_ref, dt3_ref, ecs3_ref, dte3_ref, cdec3_ref, *, heads):
    q = raw_ref.shape[0]
    lane = lax.broadcasted_iota(jnp.int32, (q, LANES), 1)
    valid = lane < heads
    pre = raw_ref[...] + bias_ref[...]
    dt = jnp.where(valid, jnp.maximum(pre, 0.0) + jnp.log(1.0 + jnp.exp(-jnp.abs(pre))), 0.0)
    d_a = dt * (-jnp.exp(alog_ref[...]))
    tri = tri_ref[...]
    cs = sum(_dot(tri, piece) for piece in _split3(d_a))
    cs = jnp.where(valid, cs, 0.0)
    cs_end = cs[q - 1:q, :]
    cs_ref[...] = cs
    cst_ref[...] = cs.T

    def stack3(v):
        return sum(_dot(piece, place_ref[j]) for j, piece in enumerate(_split3(v))).astype(BF16)

    dt3_ref[...] = stack3(dt)
    ecs3_ref[...] = stack3(jnp.exp(cs))
    dte3_ref[...] = stack3(jnp.exp(cs_end - cs))
    cdec3_ref[...] = stack3(jnp.broadcast_to(jnp.exp(cs_end), (SUBLANES, LANES)))


def _dt_quantities(dt_raw, dt_bias, a_log, q):
    rows = dt_raw.shape[0]
    heads = dt_bias.shape[0]
    assert 3 * heads <= LANES
    n_chunks = rows // q
    pad = lambda v: jnp.pad(v.astype(F32), (0, LANES - heads)).reshape(1, LANES)
    tri = (lax.broadcasted_iota(jnp.int32, (q, q), 0) >= lax.broadcasted_iota(jnp.int32, (q, q), 1)).astype(BF16)
    src = lax.broadcasted_iota(jnp.int32, (3, LANES, LANES), 1)
    dst = lax.broadcasted_iota(jnp.int32, (3, LANES, LANES), 2)
    piece = lax.broadcasted_iota(jnp.int32, (3, LANES, LANES), 0)
    place = ((src < heads) & (dst == src + piece * heads)).astype(BF16)
    row_spec = pl.BlockSpec((q, LANES), lambda i: (i, 0))
    vec_spec = pl.BlockSpec((1, LANES), lambda i: (0, 0))
    return pl.pallas_call(
        functools.partial(_dt_kernel, heads=heads),
        out_shape=(jax.ShapeDtypeStruct((rows, LANES), F32),
                   jax.ShapeDtypeStruct((LANES, rows), F32),
                   jax.ShapeDtypeStruct((rows, LANES), BF16),
                   jax.ShapeDtypeStruct((rows, LANES), BF16),
                   jax.ShapeDtypeStruct((rows, LANES), BF16),
                   jax.ShapeDtypeStruct((n_chunks, SUBLANES, LANES), BF16)),
        grid=(n_chunks,),
        in_specs=[row_spec, vec_spec, vec_spec,
                  pl.BlockSpec((q, q), lambda i: (0, 0)),
                  pl.BlockSpec((3, LANES, LANES), lambda i: (0, 0, 0))],
        out_specs=(row_spec, pl.BlockSpec((LANES, q), lambda i: (0, i)), row_spec, row_spec, row_spec,
                   pl.BlockSpec((None, SUBLANES, LANES), lambda i: (i, 0, 0))),
        compiler_params=_params(1),
        name="ssd_dt",
    )(dt_raw, pad(dt_bias), pad(a_log), tri, place)


def _ssd_kernel(xbc_ref, halo_ref, z_ref, cs_ref, cst_ref, dt3_ref, ecs3_ref, dte3_ref, cdec3_ref,
                expand_ref, cw_ref, cb_ref, dskip_ref, gnorm_ref, o_ref, xin_ref, state_ref,
                *, d_inner, groups):
    q = xbc_ref.shape[0]
    gw = d_inner // groups
    n_state = (xbc_ref.shape[1] - d_inner) // (2 * groups)
    first = pl.program_id(0) == 0

    @pl.when(first)
    def _():
        state_ref[...] = jnp.zeros_like(state_ref)

    xin_ref[0:SSM_HALO, :] = jnp.where(first, 0.0, halo_ref[...])
    xin_ref[SSM_HALO:, :] = xbc_ref[...]
    taps = cw_ref.shape[0]
    acc = jnp.broadcast_to(cb_ref[...], xbc_ref.shape)
    for k in range(taps):
        acc = acc + xin_ref[pl.ds(SSM_HALO - (taps - 1) + k, q), :] * cw_ref[pl.ds(k, 1), :]
    xbc = _silu(acc)

    expand = expand_ref[...]
    dt_x = _dot(dt3_ref[...], expand)
    ecs_x = _dot(ecs3_ref[...], expand)
    dte_x = _dot(dte3_ref[...], expand)
    cdec_x = _dot(cdec3_ref[...], expand)[0:1, :]

    xs = xbc[:, :d_inner]
    xdt = xs * dt_x
    xdt_b = xdt.astype(BF16)
    xde_b = (xdt * dte_x).astype(BF16)
    cs = cs_ref[...]
    cst = cst_ref[...]
    causal = lax.broadcasted_iota(jnp.int32, (q, q), 0) >= lax.broadcasted_iota(jnp.int32, (q, q), 1)
    low_half = lax.broadcasted_iota(jnp.int32, (q, LANES), 1) < SSM_HEAD_DIM
    heads_per_pair = LANES // SSM_HEAD_DIM

    for g in range(groups):
        cols = slice(g * gw, (g + 1) * gw)
        b_g = xbc[:, d_inner + g * n_state:d_inner + (g + 1) * n_state].astype(BF16)
        c_g = xbc[:, d_inner + (groups + g) * n_state:d_inner + (groups + g + 1) * n_state].astype(BF16)
        scores = lax.dot_general(c_g, b_g, (((1,), (1,)), ((), ())), preferred_element_type=F32)
        state = state_ref[g]
        y_off = _dot(c_g, state.astype(BF16)) * ecs_x[:, cols]
        chunk_state = lax.dot_general(b_g, xde_b[:, cols], (((0,), (0,)), ((), ())), preferred_element_type=F32)
        state_ref[g] = state * cdec_x[:, cols] + chunk_state

        pair_out = []
        for p in range(gw // LANES):
            lanes = slice(g * gw + p * LANES, g * gw + (p + 1) * LANES)
            x_pair = xdt_b[:, lanes]
            halves = []
            for e in range(heads_per_pair):
                h = (g * gw + p * LANES) // SSM_HEAD_DIM + e
                seg = cs[:, h:h + 1] - cst[h:h + 1, :]
                decay = jnp.where(causal, jnp.exp(seg), 0.0)
                halves.append(_dot((scores * decay).astype(BF16), x_pair))
            pair_out.append(jnp.where(low_half, halves[0], halves[1]))
        y_diag = jnp.concatenate(pair_out, axis=1) if len(pair_out) > 1 else pair_out[0]

        y = y_diag + y_off + dskip_ref[:, cols] * xs[:, cols]
        gated = y * _silu(z_ref[:, cols])
        inv = lax.rsqrt(jnp.mean(gated * gated, axis=-1, keepdims=True) + LN_EPS)
        o_ref[:, cols] = (gated * inv * gnorm_ref[:, cols]).astype(o_ref.dtype)


def _ssd(xbc_raw, z, dt_parts, conv_w, conv_b, d_skip, norm_g, q):
    rows, xbc_dim = xbc_raw.shape
    d_inner = z.shape[1]
    heads = d_skip.shape[0]
    n_state = (xbc_dim - d_inner) // (2 * SSM_GROUPS)
    gw = d_inner // SSM_GROUPS
    assert heads * SSM_HEAD_DIM == d_inner and gw % LANES == 0 and n_state == D_STATE
    cs, cst, dt3, ecs3, dte3, cdec3 = dt_parts
    j = lax.broadcasted_iota(jnp.int32, (LANES, d_inner), 0)
    c = lax.broadcasted_iota(jnp.int32, (LANES, d_inner), 1)
    expand = ((j < 3 * heads) & (j % heads == c // SSM_HEAD_DIM)).astype(BF16)
    dskip_x = jnp.repeat(d_skip.astype(F32), SSM_HEAD_DIM).reshape(1, d_inner)
    row = lambda w: pl.BlockSpec((q, w), lambda i: (i, 0))
    const = lambda s: pl.BlockSpec(s, lambda i: (0,) * len(s))
    halo_blocks = q // SSM_HALO
    kernel = functools.partial(_ssd_kernel, d_inner=d_inner, groups=SSM_GROUPS)
    return pl.pallas_call(
        kernel,
        out_shape=jax.ShapeDtypeStruct((rows, d_inner), BF16),
        grid=(rows // q,),
        in_specs=[row(xbc_dim),
                  pl.BlockSpec((SSM_HALO, xbc_dim), lambda i: (jnp.maximum(i * halo_blocks - 1, 0), 0)),
                  row(d_inner), row(LANES),
                  pl.BlockSpec((LANES, q), lambda i: (0, i)),
                  row(LANES), row(LANES), row(LANES),
                  pl.BlockSpec((None, SUBLANES, LANES), lambda i: (i, 0, 0)),
                  const((LANES, d_inner)), const((SSM_CONV_K, xbc_dim)), const((1, xbc_dim)),
                  const((1, d_inner)), const((1, d_inner))],
        out_specs=row(d_inner),
        scratch_shapes=[pltpu.VMEM((q + SSM_HALO, xbc_dim), F32),
                        pltpu.VMEM((SSM_GROUPS, n_state, gw), F32)],
        compiler_params=_params(1),
        name="ssd_scan",
    )(xbc_raw, xbc_raw, z, cs, cst, dt3, ecs3, dte3, cdec3, expand, conv_w,
      conv_b.reshape(1, xbc_dim), dskip_x, norm_g.reshape(1, d_inner))


def _router_kernel(x_ref, w_ref, o_ref, *, n_experts):
    logits = _dot(x_ref[...], w_ref[...].astype(BF16))
    lane = lax.broadcasted_iota(jnp.int32, logits.shape, 1).astype(F32)
    neg = -jnp.inf
    lg = jnp.where(lane < n_experts, logits, neg)
    m1 = jnp.max(lg, axis=-1, keepdims=True)
    i1 = jnp.min(jnp.where(lg == m1, lane, float(LANES)), axis=-1, keepdims=True)
    lg2 = jnp.where(lane == i1, neg, lg)
    m2 = jnp.max(lg2, axis=-1, keepdims=True)
    i2 = jnp.min(jnp.where(lg2 == m2, lane, float(LANES)), axis=-1, keepdims=True)
    e2 = jnp.exp(m2 - m1)
    w1 = 1.0 / (1.0 + e2)
    o_ref[...] = jnp.where(lane == i1, w1, 0.0) + jnp.where(lane == i2, e2 * w1, 0.0)


def _router(x_bf, w_router):
    rows, d = x_bf.shape
    n_experts = w_router.shape[1]
    w_pad = jnp.pad(w_router, ((0, 0), (0, LANES - n_experts)))
    tr = _tile(rows, 512)
    return pl.pallas_call(
        functools.partial(_router_kernel, n_experts=n_experts),
        out_shape=jax.ShapeDtypeStruct((rows, LANES), F32),
        grid=(rows // tr,),
        in_specs=[pl.BlockSpec((tr, d), lambda i: (i, 0)), pl.BlockSpec((d, LANES), lambda i: (0, 0))],
        out_specs=pl.BlockSpec((tr, LANES), lambda i: (i, 0)),
        compiler_params=_params(1),
        name="moe_router",
    )(x_bf, w_pad)


def _mixer(x_f, x_b, layer, p, alpha):
    rows, d = x_f.shape
    conv_ch = p["conv_dw"].shape[2]
    d_inner = p["ssm_norm_g"].shape[1]
    xbc_dim = p["ssm_conv_w"].shape[2]
    heads = p["dt_bias"].shape[1]
    w_in = p["w_in"]
    lead = (layer,)
    tm = _tile(rows, 1024)
    c_z = 2 * conv_ch
    c_xbc = c_z + d_inner
    c_dt = c_xbc + xbc_dim
    c_gate = c_dt + heads

    tn2 = _tile(conv_ch, 256)
    u = _matmul(_mm_glu_kernel, [(x_b, 0, d)],
                [(w_in, lead, 0, 0, d), (w_in, lead, 0, conv_ch // tn2, d)], [], conv_ch, F32,
                tm=tm, tn=tn2, name="in_proj_glu")
    z = _mm_single(_mm_plain_kernel, x_b, w_in, lead, c_z, d_inner, F32, tm=tm, tn=_tile(d_inner, 512),
                   name="in_proj_z")
    xbc_raw = _mm_single(_mm_plain_kernel, x_b, w_in, lead, c_xbc, xbc_dim, F32, tm=tm,
                         tn=_tile(xbc_dim, 512), name="in_proj_xbc")
    assert c_dt % LANES == 0
    dt_raw = _mm_single(_mm_plain_kernel, x_b, w_in, lead, c_dt, LANES, F32, tm=tm, tn=LANES,
                        name="in_proj_dt")
    w_gate = lax.slice_in_dim(w_in[layer], c_gate, c_gate + 2 * d, axis=1)
    tng = _tile(2 * d, 512)
    gates = _mm_single(_mm_sigmoid_bias_kernel, x_b, w_gate, (), 0, 2 * d, F32, tm=tm, tn=tng,
                       extra=[(p["b_gate"][layer].reshape(1, 2 * d), pl.BlockSpec((1, tng), lambda n, m: (0, n)))],
                       name="in_proj_gates")

    u_act = _conv_branch(u, p["conv_dw"][layer], p["conv_ln_g"][layer], p["conv_ln_b"][layer])

    q = _tile(rows, 256)
    dt_parts = _dt_quantities(dt_raw, p["dt_bias"][layer], p["a_log"][layer], q)
    y_act = _ssd(xbc_raw, z, dt_parts, p["ssm_conv_w"][layer], p["ssm_conv_b"][layer],
                 p["d_skip"][layer], p["ssm_norm_g"][layer], q)

    tnm = _tile(d, 512)
    gate_blocks = d // tnm
    merged = _matmul(
        _mm_merge_kernel, [(u_act, 0, conv_ch), (y_act, 0, d_inner)],
        [(p["conv_pw2"], lead, 0, 0, conv_ch), (p["ssm_out"], lead, 0, 0, d_inner)],
        [(gates, pl.BlockSpec((tm, tnm), lambda n, m: (m, n))),
         (gates, pl.BlockSpec((tm, tnm), lambda n, m: (m, n + gate_blocks)))],
        d, BF16, tm=tm, tn=tnm, name="branch_merge")

    h = _mm_single(functools.partial(_mm_residual_kernel, alpha=alpha), merged, p["w_o"], lead, 0, d, F32,
                   tm=tm, tn=tnm, extra=[(x_f, pl.BlockSpec((tm, tnm), lambda n, m: (m, n)))], name="out_proj")
    return _layer_norm(h, p["ln_mix_g"][layer], p["ln_mix_b"][layer])


DOWN_PROJ_MAX_K = 6144


def _down_proj(h_act, w, lead, x_f, alpha, name):
    rows, f = h_act.shape
    d = x_f.shape[1]
    splits = pl.cdiv(f, DOWN_PROJ_MAX_K)
    assert f % splits == 0
    kd = f // splits
    tm = _tile(rows, 512)
    tn = _tile(d, 512)
    res_spec = pl.BlockSpec((tm, tn), lambda n, m: (m, n))
    acc, scale = x_f, alpha
    for kb in range(splits):
        acc = _mm_single(functools.partial(_mm_residual_kernel, alpha=scale), h_act, w, lead, 0, d, F32,
                         tm=tm, tn=tn, extra=[(acc, res_spec)], name=f"{name}_{kb}", kb=kb, kd=kd)
        scale = 1.0
    return acc


def _dense_ffn(x_f, x_b, i, p, alpha):
    rows, d = x_f.shape
    f = p["ffn_w_gate"].shape[2]
    tm = _tile(rows, 1024)
    tn = _tile(f, 256)
    h_act = _matmul(_mm_swiglu_kernel, [(x_b, 0, d)],
                    [(p["ffn_w_gate"], (i,), 0, 0, d), (p["ffn_w_up"], (i,), 0, 0, d)], [], f, BF16,
                    tm=tm, tn=tn, name="ffn_up")
    return _down_proj(h_act, p["ffn_w_down"], (i,), x_f, alpha, "ffn_down")


def _moe_ffn(x_f, x_b, i, p, alpha):
    rows, d = x_f.shape
    n_experts, _, f = p["moe_w_gate"].shape[1:]
    comb = _router(x_b, p["moe_router"][i])
    tm = _tile(rows, 1024)
    tn = _tile(f, 256)
    tiles_per_expert = f // tn
    wspec = lambda: pl.BlockSpec((None, None, d, tn),
                                 lambda n, m: (i, n // tiles_per_expert, 0, n % tiles_per_expert))
    h_act = pl.pallas_call(
        functools.partial(_mm_swiglu_routed_kernel, tiles_per_expert=tiles_per_expert),
        out_shape=jax.ShapeDtypeStruct((rows, n_experts * f), BF16),
        grid=(n_experts * tiles_per_expert, rows // tm),
        in_specs=[pl.BlockSpec((tm, d), lambda n, m: (m, 0)), wspec(), wspec(),
                  pl.BlockSpec((tm, LANES), lambda n, m: (m, 0))],
        out_specs=pl.BlockSpec((tm, tn), lambda n, m: (m, n)),
        scratch_shapes=[pltpu.VMEM((d, tn), BF16), pltpu.VMEM((d, tn), BF16)],
        compiler_params=_params(2),
        name="moe_up",
    )(x_b, p["moe_w_gate"], p["moe_w_up"], comb)
    w_down = p["moe_w_down"].reshape(p["moe_w_down"].shape[0], n_experts * f, d)
    return _down_proj(h_act, w_down, (i,), x_f, alpha, "moe_down")


def kernel(x, w_in, b_gate, conv_dw, conv_ln_g, conv_ln_b, conv_pw2, ssm_conv_w, ssm_conv_b, dt_bias, a_log,
           d_skip, ssm_norm_g, ssm_out, w_o, ln_mix_g, ln_mix_b, ffn_w_gate, ffn_w_up, ffn_w_down, moe_router,
           moe_w_gate, moe_w_up, moe_w_down, ln_ffn_g, ln_ffn_b):
    p = dict(w_in=w_in, b_gate=b_gate, conv_dw=conv_dw, conv_ln_g=conv_ln_g, conv_ln_b=conv_ln_b,
             conv_pw2=conv_pw2, ssm_conv_w=ssm_conv_w, ssm_conv_b=ssm_conv_b, dt_bias=dt_bias, a_log=a_log,
             d_skip=d_skip, ssm_norm_g=ssm_norm_g, ssm_out=ssm_out, w_o=w_o, ln_mix_g=ln_mix_g,
             ln_mix_b=ln_mix_b, ffn_w_gate=ffn_w_gate, ffn_w_up=ffn_w_up, ffn_w_down=ffn_w_down,
             moe_router=moe_router, moe_w_gate=moe_w_gate, moe_w_up=moe_w_up, moe_w_down=moe_w_down)
    batch, seq, d = x.shape
    depth = w_in.shape[0]
    alpha = (2.0 * depth) ** 0.25
    outs = []
    for b in range(batch):
        x_f = x[b]
        x_b = x_f.astype(BF16)
        for layer in range(depth):
            x_f, x_b = _mixer(x_f, x_b, layer, p, alpha)
            i = layer // 2
            if layer % 2 == 0:
                h = _dense_ffn(x_f, x_b, i, p, alpha)
            else:
                h = _moe_ffn(x_f, x_b, i, p, alpha)
            x_f, x_b = _layer_norm(h, ln_ffn_g[layer], ln_ffn_b[layer])
        outs.append(x_f)
    return jnp.stack(outs, axis=0)
```

```python
import functools

import jax
import jax.numpy as jnp
from jax import lax
from jax.experimental import pallas as pl
from jax.experimental.pallas import tpu as pltpu

F32 = jnp.float32
BF16 = jnp.bfloat16

LANES = 128
SUBLANES = 8
VMEM_LIMIT_BYTES = 56 * 1024 * 1024

SSM_GROUPS = 4
SSM_HEAD_DIM = 64
D_STATE = 128
SSM_CONV_K = 4
TOP_K = 2
LN_EPS = 1e-5
CONV_HALO = 32
SSM_HALO = 8


def _params(n_axes):
    return pltpu.CompilerParams(dimension_semantics=("arbitrary",) * n_axes,
                                vmem_limit_bytes=VMEM_LIMIT_BYTES)


def _tile(dim, pref):
    if dim <= pref:
        return dim
    t = pref
    while dim % t:
        t -= LANES
    assert t > 0, (dim, pref)
    return t


def _sigmoid(v):
    return 1.0 / (1.0 + jnp.exp(-v))


def _silu(v):
    return v * _sigmoid(v)


def _dot(a, b):
    return jnp.dot(a, b, preferred_element_type=F32)


def _cast_weight_once(w_ref, wb_ref):
    @pl.when(pl.program_id(1) == 0)
    def _():
        if len(w_ref.shape) == 3:
            wb_ref[...] = w_ref[0].T.astype(BF16)
        else:
            wb_ref[...] = w_ref[...].astype(BF16)


def _mm_plain_kernel(x_ref, w_ref, o_ref, wb_ref):
    _cast_weight_once(w_ref, wb_ref)
    o_ref[...] = _dot(x_ref[...], wb_ref[...]).astype(o_ref.dtype)


def _mm_sigmoid_bias_kernel(x_ref, w_ref, b_ref, o_ref, wb_ref):
    _cast_weight_once(w_ref, wb_ref)
    o_ref[...] = _sigmoid(_dot(x_ref[...], wb_ref[...]) + b_ref[...]).astype(o_ref.dtype)


def _mm_residual_kernel(x_ref, w_ref, r_ref, o_ref, wb_ref, *, alpha):
    _cast_weight_once(w_ref, wb_ref)
    o_ref[...] = alpha * r_ref[...] + _dot(x_ref[...], wb_ref[...])


def _mm_glu_kernel(x_ref, wa_ref, wg_ref, o_ref, wab_ref, wgb_ref):
    _cast_weight_once(wa_ref, wab_ref)
    _cast_weight_once(wg_ref, wgb_ref)
    x = x_ref[...]
    o_ref[...] = (_dot(x, wab_ref[...]) * _sigmoid(_dot(x, wgb_ref[...]))).astype(o_ref.dtype)


def _mm_swiglu_kernel(x_ref, wg_ref, wu_ref, o_ref, wgb_ref, wub_ref):
    _cast_weight_once(wg_ref, wgb_ref)
    _cast_weight_once(wu_ref, wub_ref)
    x = x_ref[...]
    o_ref[...] = (_silu(_dot(x, wgb_ref[...])) * _dot(x, wub_ref[...])).astype(o_ref.dtype)


def _mm_swiglu_routed_kernel(x_ref, wg_ref, wu_ref, c_ref, o_ref, wgb_ref, wub_ref, *, tiles_per_expert):
    _cast_weight_once(wg_ref, wgb_ref)
    _cast_weight_once(wu_ref, wub_ref)
    x = x_ref[...]
    expert = pl.program_id(0) // tiles_per_expert
    comb = c_ref[...]
    lane = lax.broadcasted_iota(jnp.int32, comb.shape, 1)
    c = jnp.sum(jnp.where(lane == expert, comb, 0.0), axis=-1, keepdims=True)
    h = _silu(_dot(x, wgb_ref[...])) * _dot(x, wub_ref[...])
    o_ref[...] = (h * c).astype(o_ref.dtype)


def _mm_merge_kernel(u_ref, y_ref, w1_ref, w2_ref, gc_ref, gs_ref, o_ref, w1b_ref, w2b_ref):
    _cast_weight_once(w1_ref, w1b_ref)
    _cast_weight_once(w2_ref, w2b_ref)
    conv_branch = _dot(u_ref[...], w1b_ref[...])
    ssm_branch = _dot(y_ref[...], w2b_ref[...])
    o_ref[...] = (gc_ref[...] * conv_branch + gs_ref[...] * ssm_branch).astype(o_ref.dtype)


def _wspec(w, lead, k0, kdim, tn, col0):
    nlead = len(lead)
    assert w.ndim == nlead + 2
    return pl.BlockSpec((None,) * nlead + (kdim, tn), lambda n, m: tuple(lead) + (k0, col0 + n))


def _wspec_out_major(layer, row0, kdim, tn):
    assert row0 % SUBLANES == 0 and tn % SUBLANES == 0
    return pl.BlockSpec((pl.Element(1), pl.Element(tn), pl.Element(kdim)),
                        lambda n, m: (layer, pl.multiple_of(row0 + n * tn, SUBLANES), 0))


def _matmul(kernel, x_list, w_list, extra, out_cols, out_dtype, *, tm, tn, name):
    m_rows = x_list[0][0].shape[0]
    in_specs, operands, scratch = [], [], []
    for x, kb, kd in x_list:
        in_specs.append(pl.BlockSpec((tm, kd), lambda n, m, kb=kb: (m, kb)))
        operands.append(x)
    for w, spec, kd in w_list:
        in_specs.append(spec)
        operands.append(w)
        scratch.append(pltpu.VMEM((kd, tn), BF16))
    for a, spec in extra:
        in_specs.append(spec)
        operands.append(a)
    return pl.pallas_call(
        kernel,
        out_shape=jax.ShapeDtypeStruct((m_rows, out_cols), out_dtype),
        grid=(out_cols // tn, m_rows // tm),
        in_specs=in_specs,
        out_specs=pl.BlockSpec((tm, tn), lambda n, m: (m, n)),
        scratch_shapes=scratch,
        compiler_params=_params(2),
        name=name,
    )(*operands)


def _mm_single(kernel, x, w, lead, col_start, n_cols, out_dtype, *, tm, tn, extra=(), name, kb=0, kd=None):
    kd = x.shape[1] if kd is None else kd
    assert col_start % tn == 0 and n_cols % tn == 0
    return _matmul(kernel, [(x, kb, kd)], [(w, _wspec(w, lead, kb, kd, tn, col_start // tn), kd)], list(extra),
                   n_cols, out_dtype, tm=tm, tn=tn, name=name)


def _mm_out_major(kernel, x, wt, layer, row0, n_cols, out_dtype, *, tm, tn, extra=(), name):
    kd = x.shape[1]
    assert n_cols % tn == 0
    return _matmul(kernel, [(x, 0, kd)], [(wt, _wspec_out_major(layer, row0, kd, tn), kd)], list(extra),
                   n_cols, out_dtype, tm=tm, tn=tn, name=name)


def _ln_kernel(h_ref, g_ref, b_ref, of_ref, ob_ref):
    h = h_ref[...]
    mu = jnp.mean(h, axis=-1, keepdims=True)
    d = h - mu
    var = jnp.mean(d * d, axis=-1, keepdims=True)
    y = d * lax.rsqrt(var + LN_EPS) * g_ref[...] + b_ref[...]
    of_ref[...] = y
    ob_ref[...] = y.astype(BF16)


def _layer_norm(h, g, b):
    rows, d = h.shape
    tr = _tile(rows, 256)
    row_spec = pl.BlockSpec((tr, d), lambda i: (i, 0))
    vec_spec = pl.BlockSpec((1, d), lambda i: (0, 0))
    return pl.pallas_call(
        _ln_kernel,
        out_shape=(jax.ShapeDtypeStruct((rows, d), F32), jax.ShapeDtypeStruct((rows, d), BF16)),
        grid=(rows // tr,),
        in_specs=[row_spec, vec_spec, vec_spec],
        out_specs=(row_spec, row_spec),
        compiler_params=_params(1),
        name="layer_norm",
    )(h, g.reshape(1, d), b.reshape(1, d))


def _conv_branch_kernel(u_ref, up_ref, w_ref, g_ref, b_ref, o_ref, xin_ref, cv_ref, *, taps, sub_rows):
    rows, ch = u_ref.shape
    xin_ref[0:CONV_HALO, :] = jnp.where(pl.program_id(0) == 0, 0.0, up_ref[...])
    xin_ref[CONV_HALO:, :] = u_ref[...]

    lead = CONV_HALO - (taps - 1)
    phases = {}
    for k in range(taps):
        phases.setdefault((lead + k) % SUBLANES, []).append(k)

    def chunk(c, carry):
        lanes = pl.ds(pl.multiple_of(c * LANES, LANES), LANES)
        for r in range(rows // sub_rows):
            acc = jnp.zeros((sub_rows, LANES), F32)
            for b, ks in phases.items():
                span = (lead + ks[-1]) // SUBLANES * SUBLANES + sub_rows
                window = xin_ref[pl.ds(r * sub_rows + b, span), lanes]
                for k in ks:
                    a8 = (lead + k) // SUBLANES * SUBLANES
                    acc = acc + window[a8:a8 + sub_rows, :] * w_ref[pl.ds(k, 1), lanes]
            cv_ref[pl.ds(r * sub_rows, sub_rows), lanes] = acc
        return carry

    lax.fori_loop(0, ch // LANES, chunk, 0)
    v = cv_ref[...]
    mu = jnp.mean(v, axis=-1, keepdims=True)
    d = v - mu
    var = jnp.mean(d * d, axis=-1, keepdims=True)
    y = d * lax.rsqrt(var + LN_EPS) * g_ref[...] + b_ref[...]
    o_ref[...] = _silu(y).astype(o_ref.dtype)


def _conv_branch(u, conv_w, ln_g, ln_b):
    rows, ch = u.shape
    taps = conv_w.shape[0]
    assert taps - 1 <= CONV_HALO
    tr = _tile(rows, 256)
    halo_blocks = tr // CONV_HALO
    kernel = functools.partial(_conv_branch_kernel, taps=taps, sub_rows=_tile(tr, 64))
    vec_spec = pl.BlockSpec((1, ch), lambda i: (0, 0))
    return pl.pallas_call(
        kernel,
        out_shape=jax.ShapeDtypeStruct((rows, ch), BF16),
        grid=(rows // tr,),
        in_specs=[pl.BlockSpec((tr, ch), lambda i: (i, 0)),
                  pl.BlockSpec((CONV_HALO, ch), lambda i: (jnp.maximum(i * halo_blocks - 1, 0), 0)),
                  pl.BlockSpec((taps, ch), lambda i: (0, 0)),
                  vec_spec, vec_spec],
        out_specs=pl.BlockSpec((tr, ch), lambda i: (i, 0)),
        scratch_shapes=[pltpu.VMEM((tr + CONV_HALO, ch), F32), pltpu.VMEM((tr, ch), F32)],
        compiler_params=_params(1),
        name="conv_branch",
    )(u, u, conv_w, ln_g.reshape(1, ch), ln_b.reshape(1, ch))


def _split3(v):
    hi = v.astype(BF16)
    r1 = v - hi.astype(F32)
    mid = r1.astype(BF16)
    lo = (r1 - mid.astype(F32)).astype(BF16)
    return hi, mid, lo


def _dt_kernel(raw_ref, bias_ref, alog_ref, tri_ref, place_ref,
               cs_ref, cst_ref, dt3_ref, ecs3_ref, dte3_ref, cdec3_ref, *, heads):
    q = raw_ref.shape[0]
    lane = lax.broadcasted_iota(jnp.int32, (q, LANES), 1)
    valid = lane < heads
    pre = raw_ref[...] + bias_ref[...]
    dt = jnp.where(valid, jnp.maximum(pre, 0.0) + jnp.log(1.0 + jnp.exp(-jnp.abs(pre))), 0.0)
    d_a = dt * (-jnp.exp(alog_ref[...]))
    tri = tri_ref[...]
    cs = sum(_dot(tri, piece) for piece in _split3(d_a))
    cs = jnp.where(valid, cs, 0.0)
    cs_end = cs[q - 1:q, :]
    cs_ref[...] = cs
    cst_ref[...] = cs.T

    def stack3(v):
        return sum(_dot(piece, place_ref[j]) for j, piece in enumerate(_split3(v))).astype(BF16)

    dt3_ref[...] = stack3(dt)
    ecs3_ref[...] = stack3(jnp.exp(cs))
    dte3_ref[...] = stack3(jnp.exp(cs_end - cs))
    cdec3_ref[...] = stack3(jnp.broadcast_to(jnp.exp(cs_end), (SUBLANES, LANES)))


def _dt_quantities(dt_raw, dt_bias, a_log, q):
    rows = dt_raw.shape[0]
    heads = dt_bias.shape[0]
    assert 3 * heads <= LANES
    n_chunks = rows // q
    pad = lambda v: jnp.pad(v.astype(F32), (0, LANES - heads)).reshape(1, LANES)
    tri = (lax.broadcasted_iota(jnp.int32, (q, q), 0) >= lax.broadcasted_iota(jnp.int32, (q, q), 1)).astype(BF16)
    src = lax.broadcasted_iota(jnp.int32, (3, LANES, LANES), 1)
    dst = lax.broadcasted_iota(jnp.int32, (3, LANES, LANES), 2)
    piece = lax.broadcasted_iota(jnp.int32, (3, LANES, LANES), 0)
    place = ((src < heads) & (dst == src + piece * heads)).astype(BF16)
    row_spec = pl.BlockSpec((q, LANES), lambda i: (i, 0))
    vec_spec = pl.BlockSpec((1, LANES), lambda i: (0, 0))
    return pl.pallas_call(
        functools.partial(_dt_kernel, heads=heads),
        out_shape=(jax.ShapeDtypeStruct((rows, LANES), F32),
                   jax.ShapeDtypeStruct((LANES, rows), F32),
                   jax.ShapeDtypeStruct((rows, LANES), BF16),
                   jax.ShapeDtypeStruct((rows, LANES), BF16),
                   jax.ShapeDtypeStruct((rows, LANES), BF16),
                   jax.ShapeDtypeStruct((n_chunks, SUBLANES, LANES), BF16)),
        grid=(n_chunks,),
        in_specs=[row_spec, vec_spec, vec_spec,
                  pl.BlockSpec((q, q), lambda i: (0, 0)),
                  pl.BlockSpec((3, LANES, LANES), lambda i: (0, 0, 0))],
        out_specs=(row_spec, pl.BlockSpec((LANES, q), lambda i: (0, i)), row_spec, row_spec, row_spec,
                   pl.BlockSpec((None, SUBLANES, LANES), lambda i: (i, 0, 0))),
        compiler_params=_params(1),
        name="ssd_dt",
    )(dt_raw, pad(dt_bias), pad(a_log), tri, place)


def _ssd_kernel(xbc_ref, halo_ref, z_ref, cs_ref, cst_ref, dt3_ref, ecs3_ref, dte3_ref, cdec3_ref,
                expand_ref, cw_ref, cb_ref, dskip_ref, gnorm_ref, o_ref, xin_ref, state_ref,
                *, d_inner, groups):
    q = xbc_ref.shape[0]
    gw = d_inner // groups
    n_state = (xbc_ref.shape[1] - d_inner) // (2 * groups)
    first = pl.program_id(0) == 0

    @pl.when(first)
    def _():
        state_ref[...] = jnp.zeros_like(state_ref)

    xin_ref[0:SSM_HALO, :] = jnp.where(first, 0.0, halo_ref[...])
    xin_ref[SSM_HALO:, :] = xbc_ref[...]
    taps = cw_ref.shape[0]
    acc = jnp.broadcast_to(cb_ref[...], xbc_ref.shape)
    for k in range(taps):
        acc = acc + xin_ref[pl.ds(SSM_HALO - (taps - 1) + k, q), :] * cw_ref[pl.ds(k, 1), :]
    xbc = _silu(acc)

    expand = expand_ref[...]
    dt_x = _dot(dt3_ref[...], expand)
    ecs_x = _dot(ecs3_ref[...], expand)
    dte_x = _dot(dte3_ref[...], expand)
    cdec_x = _dot(cdec3_ref[...], expand)[0:1, :]

    xs = xbc[:, :d_inner]
    xdt = xs * dt_x
    xdt_b = xdt.astype(BF16)
    xde_b = (xdt * dte_x).astype(BF16)
    cs = cs_ref[...]
    cst = cst_ref[...]
    causal = lax.broadcasted_iota(jnp.int32, (q, q), 0) >= lax.broadcasted_iota(jnp.int32, (q, q), 1)
    low_half = lax.broadcasted_iota(jnp.int32, (q, LANES), 1) < SSM_HEAD_DIM
    heads_per_pair = LANES // SSM_HEAD_DIM

    for g in range(groups):
        cols = slice(g * gw, (g + 1) * gw)
        b_g = xbc[:, d_inner + g * n_state:d_inner + (g + 1) * n_state].astype(BF16)
        c_g = xbc[:, d_inner + (groups + g) * n_state:d_inner + (groups + g + 1) * n_state].astype(BF16)
        scores = lax.dot_general(c_g, b_g, (((1,), (1,)), ((), ())), preferred_element_type=F32)
        state = state_ref[g]
        y_off = _dot(c_g, state.astype(BF16)) * ecs_x[:, cols]
        chunk_state = lax.dot_general(b_g, xde_b[:, cols], (((0,), (0,)), ((), ())), preferred_element_type=F32)
        state_ref[g] = state * cdec_x[:, cols] + chunk_state

        pair_out = []
        for p in range(gw // LANES):
            lanes = slice(g * gw + p * LANES, g * gw + (p + 1) * LANES)
            x_pair = xdt_b[:, lanes]
            halves = []
            for e in range(heads_per_pair):
                h = (g * gw + p * LANES) // SSM_HEAD_DIM + e
                seg = cs[:, h:h + 1] - cst[h:h + 1, :]
                decay = jnp.where(causal, jnp.exp(seg), 0.0)
                halves.append(_dot((scores * decay).astype(BF16), x_pair))
            pair_out.append(jnp.where(low_half, halves[0], halves[1]))
        y_diag = jnp.concatenate(pair_out, axis=1) if len(pair_out) > 1 else pair_out[0]

        y = y_diag + y_off + dskip_ref[:, cols] * xs[:, cols]
        gated = y * _silu(z_ref[:, cols])
        inv = lax.rsqrt(jnp.mean(gated * gated, axis=-1, keepdims=True) + LN_EPS)
        o_ref[:, cols] = (gated * inv * gnorm_ref[:, cols]).astype(o_ref.dtype)


def _ssd(xbc_raw, z, dt_parts, conv_w, conv_b, d_skip, norm_g, q):
    rows, xbc_dim = xbc_raw.shape
    d_inner = z.shape[1]
    heads = d_skip.shape[0]
    n_state = (xbc_dim - d_inner) // (2 * SSM_GROUPS)
    gw = d_inner // SSM_GROUPS
    assert heads * SSM_HEAD_DIM == d_inner and gw % LANES == 0 and n_state == D_STATE
    cs, cst, dt3, ecs3, dte3, cdec3 = dt_parts
    j = lax.broadcasted_iota(jnp.int32, (LANES, d_inner), 0)
    c = lax.broadcasted_iota(jnp.int32, (LANES, d_inner), 1)
    expand = ((j < 3 * heads) & (j % heads == c // SSM_HEAD_DIM)).astype(BF16)
    dskip_x = jnp.repeat(d_skip.astype(F32), SSM_HEAD_DIM).reshape(1, d_inner)
    row = lambda w: pl.BlockSpec((q, w), lambda i: (i, 0))
    const = lambda s: pl.BlockSpec(s, lambda i: (0,) * len(s))
    halo_blocks = q // SSM_HALO
    kernel = functools.partial(_ssd_kernel, d_inner=d_inner, groups=SSM_GROUPS)
    return pl.pallas_call(
        kernel,
        out_shape=jax.ShapeDtypeStruct((rows, d_inner), BF16),
        grid=(rows // q,),
        in_specs=[row(xbc_dim),
                  pl.BlockSpec((SSM_HALO, xbc_dim), lambda i: (jnp.maximum(i * halo_blocks - 1, 0), 0)),
                  row(d_inner), row(LANES),
                  pl.BlockSpec((LANES, q), lambda i: (0, i)),
                  row(LANES), row(LANES), row(LANES),
                  pl.BlockSpec((None, SUBLANES, LANES), lambda i: (i, 0, 0)),
                  const((LANES, d_inner)), const((SSM_CONV_K, xbc_dim)), const((1, xbc_dim)),
                  const((1, d_inner)), const((1, d_inner))],
        out_specs=row(d_inner),
        scratch_shapes=[pltpu.VMEM((q + SSM_HALO, xbc_dim), F32),
                        pltpu.VMEM((SSM_GROUPS, n_state, gw), F32)],
        compiler_params=_params(1),
        name="ssd_scan",
    )(xbc_raw, xbc_raw, z, cs, cst, dt3, ecs3, dte3, cdec3, expand, conv_w,
      conv_b.reshape(1, xbc_dim), dskip_x, norm_g.reshape(1, d_inner))


def _router_kernel(x_ref, w_ref, o_ref, *, n_experts):
    logits = _dot(x_ref[...], w_ref[...].astype(BF16))
    lane = lax.broadcasted_iota(jnp.int32, logits.shape, 1).astype(F32)
    neg = -jnp.inf
    lg = jnp.where(lane < n_experts, logits, neg)
    m1 = jnp.max(lg, axis=-1, keepdims=True)
    i1 = jnp.min(jnp.where(lg == m1, lane, float(LANES)), axis=-1, keepdims=True)
    lg2 = jnp.where(lane == i1, neg, lg)
    m2 = jnp.max(lg2, axis=-1, keepdims=True)
    i2 = jnp.min(jnp.where(lg2 == m2, lane, float(LANES)), axis=-1, keepdims=True)
    e2 = jnp.exp(m2 - m1)
    w1 = 1.0 / (1.0 + e2)
    o_ref[...] = jnp.where(lane == i1, w1, 0.0) + jnp.where(lane == i2, e2 * w1, 0.0)


def _router(x_bf, w_router):
    rows, d = x_bf.shape
    n_experts = w_router.shape[1]
    w_pad = jnp.pad(w_router, ((0, 0), (0, LANES - n_experts)))
    tr = _tile(rows, 512)
    return pl.pallas_call(
        functools.partial(_router_kernel, n_experts=n_experts),
        out_shape=jax.ShapeDtypeStruct((rows, LANES), F32),
        grid=(rows // tr,),
        in_specs=[pl.BlockSpec((tr, d), lambda i: (i, 0)), pl.BlockSpec((d, LANES), lambda i: (0, 0))],
        out_specs=pl.BlockSpec((tr, LANES), lambda i: (i, 0)),
        compiler_params=_params(1),
        name="moe_router",
    )(x_bf, w_pad)


def _mixer(x_f, x_b, layer, p, alpha):
    rows, d = x_f.shape
    conv_ch = p["conv_dw"].shape[2]
    d_inner = p["ssm_norm_g"].shape[1]
    xbc_dim = p["ssm_conv_w"].shape[2]
    heads = p["dt_bias"].shape[1]
    w_in_t = p["w_in_t"]
    lead = (layer,)
    tm = _tile(rows, 1024)
    c_z = 2 * conv_ch
    c_xbc = c_z + d_inner
    c_dt = c_xbc + xbc_dim
    c_gate = c_dt + heads

    tn2 = _tile(conv_ch, 256)
    u = _matmul(_mm_glu_kernel, [(x_b, 0, d)],
                [(w_in_t, _wspec_out_major(layer, 0, d, tn2), d),
                 (w_in_t, _wspec_out_major(layer, conv_ch, d, tn2), d)], [], conv_ch, F32,
                tm=tm, tn=tn2, name="in_proj_glu")
    z = _mm_out_major(_mm_plain_kernel, x_b, w_in_t, layer, c_z, d_inner, F32, tm=tm, tn=_tile(d_inner, 512),
                      name="in_proj_z")
    xbc_raw = _mm_out_major(_mm_plain_kernel, x_b, w_in_t, layer, c_xbc, xbc_dim, F32, tm=tm,
                            tn=_tile(xbc_dim, 512), name="in_proj_xbc")
    dt_raw = _mm_out_major(_mm_plain_kernel, x_b, w_in_t, layer, c_dt, LANES, F32, tm=tm, tn=LANES,
                           name="in_proj_dt")
    tng = _tile(2 * d, 512)
    gates = _mm_out_major(_mm_sigmoid_bias_kernel, x_b, w_in_t, layer, c_gate, 2 * d, F32, tm=tm, tn=tng,
                          extra=[(p["b_gate"][layer].reshape(1, 2 * d), pl.BlockSpec((1, tng), lambda n, m: (0, n)))],
                          name="in_proj_gates")

    u_act = _conv_branch(u, p["conv_dw"][layer], p["conv_ln_g"][layer], p["conv_ln_b"][layer])

    q = _tile(rows, 256)
    dt_parts = _dt_quantities(dt_raw, p["dt_bias"][layer], p["a_log"][layer], q)
    y_act = _ssd(xbc_raw, z, dt_parts, p["ssm_conv_w"][layer], p["ssm_conv_b"][layer],
                 p["d_skip"][layer], p["ssm_norm_g"][layer], q)

    tnm = _tile(d, 512)
    gate_blocks = d // tnm
    merged = _matmul(
        _mm_merge_kernel, [(u_act, 0, conv_ch), (y_act, 0, d_inner)],
        [(p["conv_pw2"], _wspec(p["conv_pw2"], lead, 0, conv_ch, tnm, 0), conv_ch),
         (p["ssm_out"], _wspec(p["ssm_out"], lead, 0, d_inner, tnm, 0), d_inner)],
        [(gates, pl.BlockSpec((tm, tnm), lambda n, m: (m, n))),
         (gates, pl.BlockSpec((tm, tnm), lambda n, m: (m, n + gate_blocks)))],
        d, BF16, tm=tm, tn=tnm, name="branch_merge")

    h = _mm_single(functools.partial(_mm_residual_kernel, alpha=alpha), merged, p["w_o"], lead, 0, d, F32,
                   tm=tm, tn=tnm, extra=[(x_f, pl.BlockSpec((tm, tnm), lambda n, m: (m, n)))], name="out_proj")
    return _layer_norm(h, p["ln_mix_g"][layer], p["ln_mix_b"][layer])


DOWN_PROJ_MAX_K = 6144


def _down_proj(h_act, w, lead, x_f, alpha, name):
    rows, f = h_act.shape
    d = x_f.shape[1]
    splits = pl.cdiv(f, DOWN_PROJ_MAX_K)
    assert f % splits == 0
    kd = f // splits
    tm = _tile(rows, 512)
    tn = _tile(d, 512)
    res_spec = pl.BlockSpec((tm, tn), lambda n, m: (m, n))
    acc, scale = x_f, alpha
    for kb in range(splits):
        acc = _mm_single(functools.partial(_mm_residual_kernel, alpha=scale), h_act, w, lead, 0, d, F32,
                         tm=tm, tn=tn, extra=[(acc, res_spec)], name=f"{name}_{kb}", kb=kb, kd=kd)
        scale = 1.0
    return acc


def _dense_ffn(x_f, x_b, i, p, alpha):
    rows, d = x_f.shape
    f = p["ffn_w_gate"].shape[2]
    tm = _tile(rows, 1024)
    tn = _tile(f, 256)
    h_act = _matmul(_mm_swiglu_kernel, [(x_b, 0, d)],
                    [(p["ffn_w_gate"], _wspec(p["ffn_w_gate"], (i,), 0, d, tn, 0), d),
                     (p["ffn_w_up"], _wspec(p["ffn_w_up"], (i,), 0, d, tn, 0), d)], [], f, BF16,
                    tm=tm, tn=tn, name="ffn_up")
    return _down_proj(h_act, p["ffn_w_down"], (i,), x_f, alpha, "ffn_down")


def _moe_ffn(x_f, x_b, i, p, alpha):
    rows, d = x_f.shape
    n_experts, _, f = p["moe_w_gate"].shape[1:]
    comb = _router(x_b, p["moe_router"][i])
    tm = _tile(rows, 1024)
    tn = _tile(f, 256)
    tiles_per_expert = f // tn
    wspec = lambda: pl.BlockSpec((None, None, d, tn),
                                 lambda n, m: (i, n // tiles_per_expert, 0, n % tiles_per_expert))
    h_act = pl.pallas_call(
        functools.partial(_mm_swiglu_routed_kernel, tiles_per_expert=tiles_per_expert),
        out_shape=jax.ShapeDtypeStruct((rows, n_experts * f), BF16),
        grid=(n_experts * tiles_per_expert, rows // tm),
        in_specs=[pl.BlockSpec((tm, d), lambda n, m: (m, 0)), wspec(), wspec(),
                  pl.BlockSpec((tm, LANES), lambda n, m: (m, 0))],
        out_specs=pl.BlockSpec((tm, tn), lambda n, m: (m, n)),
        scratch_shapes=[pltpu.VMEM((d, tn), BF16), pltpu.VMEM((d, tn), BF16)],
        compiler_params=_params(2),
        name="moe_up",
    )(x_b, p["moe_w_gate"], p["moe_w_up"], comb)
    w_down = p["moe_w_down"].reshape(p["moe_w_down"].shape[0], n_experts * f, d)
    return _down_proj(h_act, w_down, (i,), x_f, alpha, "moe_down")


def kernel(x, w_in, b_gate, conv_dw, conv_ln_g, conv_ln_b, conv_pw2, ssm_conv_w, ssm_conv_b, dt_bias, a_log,
           d_skip, ssm_norm_g, ssm_out, w_o, ln_mix_g, ln_mix_b, ffn_w_gate, ffn_w_up, ffn_w_down, moe_router,
           moe_w_gate, moe_w_up, moe_w_down, ln_ffn_g, ln_ffn_b):
    p = dict(w_in_t=jnp.swapaxes(w_in, 1, 2), b_gate=b_gate, conv_dw=conv_dw, conv_ln_g=conv_ln_g, conv_ln_b=conv_ln_b,
             conv_pw2=conv_pw2, ssm_conv_w=ssm_conv_w, ssm_conv_b=ssm_conv_b, dt_bias=dt_bias, a_log=a_log,
             d_skip=d_skip, ssm_norm_g=ssm_norm_g, ssm_out=ssm_out, w_o=w_o, ln_mix_g=ln_mix_g,
             ln_mix_b=ln_mix_b, ffn_w_gate=ffn_w_gate, ffn_w_up=ffn_w_up, ffn_w_down=ffn_w_down,
             moe_router=moe_router, moe_w_gate=moe_w_gate, moe_w_up=moe_w_up, moe_w_down=moe_w_down)
    batch, seq, d = x.shape
    depth = w_in.shape[0]
    alpha = (2.0 * depth) ** 0.25
    outs = []
    for b in range(batch):
        x_f = x[b]
        x_b = x_f.astype(BF16)
        for layer in range(depth):
            x_f, x_b = _mixer(x_f, x_b, layer, p, alpha)
            i = layer // 2
            if layer % 2 == 0:
                h = _dense_ffn(x_f, x_b, i, p, alpha)
            else:
                h = _moe_ffn(x_f, x_b, i, p, alpha)
            x_f, x_b = _layer_norm(h, ln_ffn_g[layer], ln_ffn_b[layer])
        outs.append(x_f)
    return jnp.stack(outs, axis=0)
```

```python
import functools

import jax
import jax.numpy as jnp
from jax import lax
from jax.experimental import pallas as pl
from jax.experimental.pallas import tpu as pltpu

F32 = jnp.float32
BF16 = jnp.bfloat16

LANES = 128
SUBLANES = 8
VMEM_LIMIT_BYTES = 56 * 1024 * 1024

SSM_GROUPS = 4
SSM_HEAD_DIM = 64
D_STATE = 128
SSM_CONV_K = 4
TOP_K = 2
LN_EPS = 1e-5
CONV_HALO = 32
SSM_HALO = 8


def _params(n_axes):
    return pltpu.CompilerParams(dimension_semantics=("arbitrary",) * n_axes,
                                vmem_limit_bytes=VMEM_LIMIT_BYTES)


def _tile(dim, pref):
    if dim <= pref:
        return dim
    t = pref
    while dim % t:
        t -= LANES
    assert t > 0, (dim, pref)
    return t


def _sigmoid(v):
    return 1.0 / (1.0 + jnp.exp(-v))


def _silu(v):
    return v * _sigmoid(v)


def _dot(a, b):
    return jnp.dot(a, b, preferred_element_type=F32)


def _cast_weight_once(w_ref, wb_ref):
    @pl.when(pl.program_id(1) == 0)
    def _():
        if len(w_ref.shape) == 3:
            wb_ref[...] = w_ref[0].T.astype(BF16)
        else:
            wb_ref[...] = w_ref[...].astype(BF16)


def _mm_plain_kernel(x_ref, w_ref, o_ref, wb_ref):
    _cast_weight_once(w_ref, wb_ref)
    o_ref[...] = _dot(x_ref[...], wb_ref[...]).astype(o_ref.dtype)


def _mm_sigmoid_bias_kernel(x_ref, w_ref, b_ref, o_ref, wb_ref):
    _cast_weight_once(w_ref, wb_ref)
    o_ref[...] = _sigmoid(_dot(x_ref[...], wb_ref[...]) + b_ref[...]).astype(o_ref.dtype)


def _mm_residual_kernel(x_ref, w_ref, r_ref, o_ref, wb_ref, *, alpha):
    _cast_weight_once(w_ref, wb_ref)
    o_ref[...] = alpha * r_ref[...] + _dot(x_ref[...], wb_ref[...])


def _mm_glu_kernel(x_ref, wa_ref, wg_ref, o_ref, wab_ref, wgb_ref):
    _cast_weight_once(wa_ref, wab_ref)
    _cast_weight_once(wg_ref, wgb_ref)
    x = x_ref[...]
    o_ref[...] = (_dot(x, wab_ref[...]) * _sigmoid(_dot(x, wgb_ref[...]))).astype(o_ref.dtype)


def _mm_swiglu_kernel(x_ref, wg_ref, wu_ref, o_ref, wgb_ref, wub_ref):
    _cast_weight_once(wg_ref, wgb_ref)
    _cast_weight_once(wu_ref, wub_ref)
    x = x_ref[...]
    o_ref[...] = (_silu(_dot(x, wgb_ref[...])) * _dot(x, wub_ref[...])).astype(o_ref.dtype)


def _mm_merge_kernel(u_ref, y_ref, w1_ref, w2_ref, gc_ref, gs_ref, o_ref, w1b_ref, w2b_ref):
    _cast_weight_once(w1_ref, w1b_ref)
    _cast_weight_once(w2_ref, w2b_ref)
    conv_branch = _dot(u_ref[...], w1b_ref[...])
    ssm_branch = _dot(y_ref[...], w2b_ref[...])
    o_ref[...] = (gc_ref[...] * conv_branch + gs_ref[...] * ssm_branch).astype(o_ref.dtype)


def _wspec(w, lead, k0, kdim, tn, col0):
    nlead = len(lead)
    assert w.ndim == nlead + 2
    return pl.BlockSpec((None,) * nlead + (kdim, tn), lambda n, m: tuple(lead) + (k0, col0 + n))


def _wspec_out_major(layer, row0, kdim, tn):
    assert row0 % SUBLANES == 0 and tn % SUBLANES == 0
    return pl.BlockSpec((pl.Element(1), pl.Element(tn), pl.Element(kdim)),
                        lambda n, m: (layer, pl.multiple_of(row0 + n * tn, SUBLANES), 0))


def _matmul(kernel, x_list, w_list, extra, out_cols, out_dtype, *, tm, tn, name):
    m_rows = x_list[0][0].shape[0]
    in_specs, operands, scratch = [], [], []
    for x, kb, kd in x_list:
        in_specs.append(pl.BlockSpec((tm, kd), lambda n, m, kb=kb: (m, kb)))
        operands.append(x)
    for w, spec, kd in w_list:
        in_specs.append(spec)
        operands.append(w)
        scratch.append(pltpu.VMEM((kd, tn), BF16))
    for a, spec in extra:
        in_specs.append(spec)
        operands.append(a)
    return pl.pallas_call(
        kernel,
        out_shape=jax.ShapeDtypeStruct((m_rows, out_cols), out_dtype),
        grid=(out_cols // tn, m_rows // tm),
        in_specs=in_specs,
        out_specs=pl.BlockSpec((tm, tn), lambda n, m: (m, n)),
        scratch_shapes=scratch,
        compiler_params=_params(2),
        name=name,
    )(*operands)


def _mm_single(kernel, x, w, lead, col_start, n_cols, out_dtype, *, tm, tn, extra=(), name, kb=0, kd=None):
    kd = x.shape[1] if kd is None else kd
    assert col_start % tn == 0 and n_cols % tn == 0
    return _matmul(kernel, [(x, kb, kd)], [(w, _wspec(w, lead, kb, kd, tn, col_start // tn), kd)], list(extra),
                   n_cols, out_dtype, tm=tm, tn=tn, name=name)


def _mm_out_major(kernel, x, wt, layer, row0, n_cols, out_dtype, *, tm, tn, extra=(), name):
    kd = x.shape[1]
    assert n_cols % tn == 0
    return _matmul(kernel, [(x, 0, kd)], [(wt, _wspec_out_major(layer, row0, kd, tn), kd)], list(extra),
                   n_cols, out_dtype, tm=tm, tn=tn, name=name)


def _pack_bf16_pairs(y):
    half = y.shape[1] // 2
    bits = lambda v: lax.bitcast_convert_type(v.astype(BF16).astype(F32), jnp.uint32)
    return (bits(y[:, :half]) >> 16) | bits(y[:, half:])


def _unpack_bf16_pairs(pk):
    lo = lax.bitcast_convert_type(pk << 16, F32).astype(BF16)
    hi = lax.bitcast_convert_type(pk & jnp.uint32(0xFFFF0000), F32).astype(BF16)
    return lo, hi


def _ln_kernel(h_ref, g_ref, b_ref, of_ref, ob_ref, *maybe_packed_ref):
    h = h_ref[...]
    mu = jnp.mean(h, axis=-1, keepdims=True)
    d = h - mu
    var = jnp.mean(d * d, axis=-1, keepdims=True)
    y = d * lax.rsqrt(var + LN_EPS) * g_ref[...] + b_ref[...]
    of_ref[...] = y
    ob_ref[...] = y.astype(BF16)
    for pk_ref in maybe_packed_ref:
        pk_ref[...] = _pack_bf16_pairs(y)


def _layer_norm(h, g, b, with_packed=False):
    rows, d = h.shape
    tr = _tile(rows, 256)
    row_spec = pl.BlockSpec((tr, d), lambda i: (i, 0))
    vec_spec = pl.BlockSpec((1, d), lambda i: (0, 0))
    out_shape = [jax.ShapeDtypeStruct((rows, d), F32), jax.ShapeDtypeStruct((rows, d), BF16)]
    out_specs = [row_spec, row_spec]
    if with_packed:
        out_shape.append(jax.ShapeDtypeStruct((rows, d // 2), jnp.uint32))
        out_specs.append(pl.BlockSpec((tr, d // 2), lambda i: (i, 0)))
    return pl.pallas_call(
        _ln_kernel,
        out_shape=tuple(out_shape),
        grid=(rows // tr,),
        in_specs=[row_spec, vec_spec, vec_spec],
        out_specs=tuple(out_specs),
        compiler_params=_params(1),
        name="layer_norm",
    )(h, g.reshape(1, d), b.reshape(1, d))


def _conv_branch_kernel(u_ref, up_ref, w_ref, g_ref, b_ref, o_ref, xin_ref, cv_ref, *, taps, sub_rows):
    rows, ch = u_ref.shape
    xin_ref[0:CONV_HALO, :] = jnp.where(pl.program_id(0) == 0, 0.0, up_ref[...])
    xin_ref[CONV_HALO:, :] = u_ref[...]

    lead = CONV_HALO - (taps - 1)
    phases = {}
    for k in range(taps):
        phases.setdefault((lead + k) % SUBLANES, []).append(k)

    def chunk(c, carry):
        lanes = pl.ds(pl.multiple_of(c * LANES, LANES), LANES)
        for r in range(rows // sub_rows):
            acc = jnp.zeros((sub_rows, LANES), F32)
            for b, ks in phases.items():
                span = (lead + ks[-1]) // SUBLANES * SUBLANES + sub_rows
                window = xin_ref[pl.ds(r * sub_rows + b, span), lanes]
                for k in ks:
                    a8 = (lead + k) // SUBLANES * SUBLANES
                    acc = acc + window[a8:a8 + sub_rows, :] * w_ref[pl.ds(k, 1), lanes]
            cv_ref[pl.ds(r * sub_rows, sub_rows), lanes] = acc
        return carry

    lax.fori_loop(0, ch // LANES, chunk, 0)
    v = cv_ref[...]
    mu = jnp.mean(v, axis=-1, keepdims=True)
    d = v - mu
    var = jnp.mean(d * d, axis=-1, keepdims=True)
    y = d * lax.rsqrt(var + LN_EPS) * g_ref[...] + b_ref[...]
    o_ref[...] = _silu(y).astype(o_ref.dtype)


def _conv_branch(u, conv_w, ln_g, ln_b):
    rows, ch = u.shape
    taps = conv_w.shape[0]
    assert taps - 1 <= CONV_HALO
    tr = _tile(rows, 256)
    halo_blocks = tr // CONV_HALO
    kernel = functools.partial(_conv_branch_kernel, taps=taps, sub_rows=_tile(tr, 64))
    vec_spec = pl.BlockSpec((1, ch), lambda i: (0, 0))
    return pl.pallas_call(
        kernel,
        out_shape=jax.ShapeDtypeStruct((rows, ch), BF16),
        grid=(rows // tr,),
        in_specs=[pl.BlockSpec((tr, ch), lambda i: (i, 0)),
                  pl.BlockSpec((CONV_HALO, ch), lambda i: (jnp.maximum(i * halo_blocks - 1, 0), 0)),
                  pl.BlockSpec((taps, ch), lambda i: (0, 0)),
                  vec_spec, vec_spec],
        out_specs=pl.BlockSpec((tr, ch), lambda i: (i, 0)),
        scratch_shapes=[pltpu.VMEM((tr + CONV_HALO, ch), F32), pltpu.VMEM((tr, ch), F32)],
        compiler_params=_params(1),
        name="conv_branch",
    )(u, u, conv_w, ln_g.reshape(1, ch), ln_b.reshape(1, ch))


def _split3(v):
    hi = v.astype(BF16)
    r1 = v - hi.astype(F32)
    mid = r1.astype(BF16)
    lo = (r1 - mid.astype(F32)).astype(BF16)
    return hi, mid, lo


def _dt_kernel(raw_ref, bias_ref, alog_ref, tri_ref, place_ref,
               cs_ref, cst_ref, dt3_ref, ecs3_ref, dte3_ref, cdec3_ref, *, heads):
    q = raw_ref.shape[0]
    lane = lax.broadcasted_iota(jnp.int32, (q, LANES), 1)
    valid = lane < heads
    pre = raw_ref[...] + bias_ref[...]
    dt = jnp.where(valid, jnp.maximum(pre, 0.0) + jnp.log(1.0 + jnp.exp(-jnp.abs(pre))), 0.0)
    d_a = dt * (-jnp.exp(alog_ref[...]))
    tri = tri_ref[...]
    cs = sum(_dot(tri, piece) for piece in _split3(d_a))
    cs = jnp.where(valid, cs, 0.0)
    cs_end = cs[q - 1:q, :]
    cs_ref[...] = cs
    cst_ref[...] = cs.T

    def stack3(v):
        return sum(_dot(piece, place_ref[j]) for j, piece in enumerate(_split3(v))).astype(BF16)

    dt3_ref[...] = stack3(dt)
    ecs3_ref[...] = stack3(jnp.exp(cs))
    dte3_ref[...] = stack3(jnp.exp(cs_end - cs))
    cdec3_ref[...] = stack3(jnp.broadcast_to(jnp.exp(cs_end), (SUBLANES, LANES)))


def _dt_quantities(dt_raw, dt_bias, a_log, q):
    rows = dt_raw.shape[0]
    heads = dt_bias.shape[0]
    assert 3 * heads <= LANES
    n_chunks = rows // q
    pad = lambda v: jnp.pad(v.astype(F32), (0, LANES - heads)).reshape(1, LANES)
    tri = (lax.broadcasted_iota(jnp.int32, (q, q), 0) >= lax.broadcasted_iota(jnp.int32, (q, q), 1)).astype(BF16)
    src = lax.broadcasted_iota(jnp.int32, (3, LANES, LANES), 1)
    dst = lax.broadcasted_iota(jnp.int32, (3, LANES, LANES), 2)
    piece = lax.broadcasted_iota(jnp.int32, (3, LANES, LANES), 0)
    place = ((src < heads) & (dst == src + piece * heads)).astype(BF16)
    row_spec = pl.BlockSpec((q, LANES), lambda i: (i, 0))
    vec_spec = pl.BlockSpec((1, LANES), lambda i: (0, 0))
    return pl.pallas_call(
        functools.partial(_dt_kernel, heads=heads),
        out_shape=(jax.ShapeDtypeStruct((rows, LANES), F32),
                   jax.ShapeDtypeStruct((LANES, rows), F32),
                   jax.ShapeDtypeStruct((rows, LANES), BF16),
                   jax.ShapeDtypeStruct((rows, LANES), BF16),
                   jax.ShapeDtypeStruct((rows, LANES), BF16),
                   jax.ShapeDtypeStruct((n_chunks, SUBLANES, LANES), BF16)),
        grid=(n_chunks,),
        in_specs=[row_spec, vec_spec, vec_spec,
                  pl.BlockSpec((q, q), lambda i: (0, 0)),
                  pl.BlockSpec((3, LANES, LANES), lambda i: (0, 0, 0))],
        out_specs=(row_spec, pl.BlockSpec((LANES, q), lambda i: (0, i)), row_spec, row_spec, row_spec,
                   pl.BlockSpec((None, SUBLANES, LANES), lambda i: (i, 0, 0))),
        compiler_params=_params(1),
        name="ssd_dt",
    )(dt_raw, pad(dt_bias), pad(a_log), tri, place)


def _ssd_kernel(xbc_ref, halo_ref, z_ref, cs_ref, cst_ref, dt3_ref, ecs3_ref, dte3_ref, cdec3_ref,
                expand_ref, cw_ref, cb_ref, dskip_ref, gnorm_ref, o_ref, xin_ref, state_ref,
                *, d_inner, groups):
    q = xbc_ref.shape[0]
    gw = d_inner // groups
    n_state = (xbc_ref.shape[1] - d_inner) // (2 * groups)
    first = pl.program_id(0) == 0

    @pl.when(first)
    def _():
        state_ref[...] = jnp.zeros_like(state_ref)

    xin_ref[0:SSM_HALO, :] = jnp.where(first, 0.0, halo_ref[...])
    xin_ref[SSM_HALO:, :] = xbc_ref[...]
    taps = cw_ref.shape[0]
    acc = jnp.broadcast_to(cb_ref[...], xbc_ref.shape)
    for k in range(taps):
        acc = acc + xin_ref[pl.ds(SSM_HALO - (taps - 1) + k, q), :] * cw_ref[pl.ds(k, 1), :]
    xbc = _silu(acc)

    expand = expand_ref[...]
    dt_x = _dot(dt3_ref[...], expand)
    ecs_x = _dot(ecs3_ref[...], expand)
    dte_x = _dot(dte3_ref[...], expand)
    cdec_x = _dot(cdec3_ref[...], expand)[0:1, :]

    xs = xbc[:, :d_inner]
    xdt = xs * dt_x
    xdt_b = xdt.astype(BF16)
    xde_b = (xdt * dte_x).astype(BF16)
    cs = cs_ref[...]
    cst = cst_ref[...]
    causal = lax.broadcasted_iota(jnp.int32, (q, q), 0) >= lax.broadcasted_iota(jnp.int32, (q, q), 1)
    low_half = lax.broadcasted_iota(jnp.int32, (q, LANES), 1) < SSM_HEAD_DIM
    heads_per_pair = LANES // SSM_HEAD_DIM

    for g in range(groups):
        cols = slice(g * gw, (g + 1) * gw)
        b_g = xbc[:, d_inner + g * n_state:d_inner + (g + 1) * n_state].astype(BF16)
        c_g = xbc[:, d_inner + (groups + g) * n_state:d_inner + (groups + g + 1) * n_state].astype(BF16)
        scores = lax.dot_general(c_g, b_g, (((1,), (1,)), ((), ())), preferred_element_type=F32)
        state = state_ref[g]
        y_off = _dot(c_g, state.astype(BF16)) * ecs_x[:, cols]
        chunk_state = lax.dot_general(b_g, xde_b[:, cols], (((0,), (0,)), ((), ())), preferred_element_type=F32)
        state_ref[g] = state * cdec_x[:, cols] + chunk_state

        pair_out = []
        for p in range(gw // LANES):
            lanes = slice(g * gw + p * LANES, g * gw + (p + 1) * LANES)
            x_pair = xdt_b[:, lanes]
            halves = []
            for e in range(heads_per_pair):
                h = (g * gw + p * LANES) // SSM_HEAD_DIM + e
                seg = cs[:, h:h + 1] - cst[h:h + 1, :]
                decay = jnp.where(causal, jnp.exp(seg), 0.0)
                halves.append(_dot((scores * decay).astype(BF16), x_pair))
            pair_out.append(jnp.where(low_half, halves[0], halves[1]))
        y_diag = jnp.concatenate(pair_out, axis=1) if len(pair_out) > 1 else pair_out[0]

        y = y_diag + y_off + dskip_ref[:, cols] * xs[:, cols]
        gated = y * _silu(z_ref[:, cols])
        inv = lax.rsqrt(jnp.mean(gated * gated, axis=-1, keepdims=True) + LN_EPS)
        o_ref[:, cols] = (gated * inv * gnorm_ref[:, cols]).astype(o_ref.dtype)


def _ssd(xbc_raw, z, dt_parts, conv_w, conv_b, d_skip, norm_g, q):
    rows, xbc_dim = xbc_raw.shape
    d_inner = z.shape[1]
    heads = d_skip.shape[0]
    n_state = (xbc_dim - d_inner) // (2 * SSM_GROUPS)
    gw = d_inner // SSM_GROUPS
    assert heads * SSM_HEAD_DIM == d_inner and gw % LANES == 0 and n_state == D_STATE
    cs, cst, dt3, ecs3, dte3, cdec3 = dt_parts
    j = lax.broadcasted_iota(jnp.int32, (LANES, d_inner), 0)
    c = lax.broadcasted_iota(jnp.int32, (LANES, d_inner), 1)
    expand = ((j < 3 * heads) & (j % heads == c // SSM_HEAD_DIM)).astype(BF16)
    dskip_x = jnp.repeat(d_skip.astype(F32), SSM_HEAD_DIM).reshape(1, d_inner)
    row = lambda w: pl.BlockSpec((q, w), lambda i: (i, 0))
    const = lambda s: pl.BlockSpec(s, lambda i: (0,) * len(s))
    halo_blocks = q // SSM_HALO
    kernel = functools.partial(_ssd_kernel, d_inner=d_inner, groups=SSM_GROUPS)
    return pl.pallas_call(
        kernel,
        out_shape=jax.ShapeDtypeStruct((rows, d_inner), BF16),
        grid=(rows // q,),
        in_specs=[row(xbc_dim),
                  pl.BlockSpec((SSM_HALO, xbc_dim), lambda i: (jnp.maximum(i * halo_blocks - 1, 0), 0)),
                  row(d_inner), row(LANES),
                  pl.BlockSpec((LANES, q), lambda i: (0, i)),
                  row(LANES), row(LANES), row(LANES),
                  pl.BlockSpec((None, SUBLANES, LANES), lambda i: (i, 0, 0)),
                  const((LANES, d_inner)), const((SSM_CONV_K, xbc_dim)), const((1, xbc_dim)),
                  const((1, d_inner)), const((1, d_inner))],
        out_specs=row(d_inner),
        scratch_shapes=[pltpu.VMEM((q + SSM_HALO, xbc_dim), F32),
                        pltpu.VMEM((SSM_GROUPS, n_state, gw), F32)],
        compiler_params=_params(1),
        name="ssd_scan",
    )(xbc_raw, xbc_raw, z, cs, cst, dt3, ecs3, dte3, cdec3, expand, conv_w,
      conv_b.reshape(1, xbc_dim), dskip_x, norm_g.reshape(1, d_inner))


MOE_ROW_TILE = 512
MOE_TOKEN_TILE = 256


def _router_kernel(x_ref, w_ref, tri_ref, wgt_ref, sel_ref, cnt_ref, carry_ref, *, n_experts):
    @pl.when(pl.program_id(0) == 0)
    def _():
        carry_ref[...] = jnp.zeros_like(carry_ref)

    logits = _dot(x_ref[...], w_ref[...].astype(BF16))
    lane_i = lax.broadcasted_iota(jnp.int32, logits.shape, 1)
    lane = lane_i.astype(F32)
    neg = -jnp.inf
    lg = jnp.where(lane_i < n_experts, logits, neg)
    m1 = jnp.max(lg, axis=-1, keepdims=True)
    i1 = jnp.min(jnp.where(lg == m1, lane, float(LANES)), axis=-1, keepdims=True)
    lg2 = jnp.where(lane == i1, neg, lg)
    m2 = jnp.max(lg2, axis=-1, keepdims=True)
    i2 = jnp.min(jnp.where(lg2 == m2, lane, float(LANES)), axis=-1, keepdims=True)
    e2 = jnp.exp(m2 - m1)
    w1 = 1.0 / (1.0 + e2)
    wgt_ref[...] = jnp.where(lane_i == 0, w1, jnp.where(lane_i == 1, e2 * w1, 0.0))

    sel1 = lane == i1
    sel2 = lane == i2
    onehot = jnp.where(sel1, 1.0, jnp.where(sel2, 1.0, 0.0))
    earlier = _dot(tri_ref[...], onehot.astype(BF16)) + carry_ref[0:1, :]
    rank1 = jnp.sum(jnp.where(sel1, earlier, 0.0), axis=-1, keepdims=True)
    rank2 = jnp.sum(jnp.where(sel2, earlier, 0.0), axis=-1, keepdims=True)
    sel = jnp.where(lane_i == 0, i1, jnp.where(lane_i == 1, i2,
                    jnp.where(lane_i == 2, rank1, jnp.where(lane_i == 3, rank2, 0.0))))
    sel_ref[...] = sel.astype(jnp.int32)
    carry_ref[...] = carry_ref[...] + jnp.sum(onehot, axis=0, keepdims=True)
    cnt_ref[...] = carry_ref[...].astype(jnp.int32)


def _router(x_bf, w_router):
    rows, d = x_bf.shape
    n_experts = w_router.shape[1]
    w_pad = jnp.pad(w_router, ((0, 0), (0, LANES - n_experts)))
    tr = _tile(rows, 512)
    tri = (lax.broadcasted_iota(jnp.int32, (tr, tr), 0) > lax.broadcasted_iota(jnp.int32, (tr, tr), 1)).astype(BF16)
    row_spec = pl.BlockSpec((tr, LANES), lambda i: (i, 0))
    return pl.pallas_call(
        functools.partial(_router_kernel, n_experts=n_experts),
        out_shape=(jax.ShapeDtypeStruct((rows, LANES), F32), jax.ShapeDtypeStruct((rows, LANES), jnp.int32),
                   jax.ShapeDtypeStruct((SUBLANES, LANES), jnp.int32)),
        grid=(rows // tr,),
        in_specs=[pl.BlockSpec((tr, d), lambda i: (i, 0)), pl.BlockSpec((d, LANES), lambda i: (0, 0)),
                  pl.BlockSpec((tr, tr), lambda i: (0, 0))],
        out_specs=(row_spec, row_spec, pl.BlockSpec((SUBLANES, LANES), lambda i: (0, 0))),
        scratch_shapes=[pltpu.VMEM((SUBLANES, LANES), F32)],
        compiler_params=_params(1),
        name="moe_router",
    )(x_bf, w_pad, tri)


def _row_copy(src, src_row, dst, dst_row, sem):
    return pltpu.make_async_copy(src.at[pl.ds(src_row, 1), :], dst.at[pl.ds(dst_row, 1), :], sem)


def _gather_kernel(pos1_ref, pos2_ref, zfrom_ref, zcount_ref, x_hbm, xg_hbm, zero_ref, sem_ref, zsem_ref,
                   *, tokens, n_experts):
    base = pl.program_id(0) * tokens

    def start(t, carry):
        _row_copy(x_hbm, base + t, xg_hbm, pos1_ref[base + t], sem_ref).start()
        _row_copy(x_hbm, base + t, xg_hbm, pos2_ref[base + t], sem_ref).start()
        return carry

    def wait(t, carry):
        _row_copy(x_hbm, 0, xg_hbm, 0, sem_ref).wait()
        _row_copy(x_hbm, 0, xg_hbm, 0, sem_ref).wait()
        return carry

    lax.fori_loop(0, tokens, start, 0)

    @pl.when(pl.program_id(0) == 0)
    def _():
        zero_ref[...] = jnp.zeros_like(zero_ref)
        for e in range(n_experts):
            def zstart(j, carry, e=e):
                _row_copy(zero_ref, 0, xg_hbm, zfrom_ref[e] + j, zsem_ref).start()
                return carry

            def zwait(j, carry):
                _row_copy(zero_ref, 0, xg_hbm, 0, zsem_ref).wait()
                return carry

            lax.fori_loop(0, zcount_ref[e], zstart, 0)
            lax.fori_loop(0, zcount_ref[e], zwait, 0)

    lax.fori_loop(0, tokens, wait, 0)


def _moe_gather(x_pk, pos1, pos2, zero_from, zero_count, rows_g):
    rows, w = x_pk.shape
    tokens = _tile(rows, MOE_TOKEN_TILE)
    kernel = functools.partial(_gather_kernel, tokens=tokens, n_experts=zero_from.shape[0])
    return pl.pallas_call(
        kernel,
        out_shape=jax.ShapeDtypeStruct((rows_g, w), x_pk.dtype),
        grid_spec=pltpu.PrefetchScalarGridSpec(
            num_scalar_prefetch=4, grid=(rows // tokens,),
            in_specs=[pl.BlockSpec(memory_space=pl.ANY)],
            out_specs=pl.BlockSpec(memory_space=pl.ANY),
            scratch_shapes=[pltpu.VMEM((SUBLANES, w), x_pk.dtype), pltpu.SemaphoreType.DMA(()),
                            pltpu.SemaphoreType.DMA(())]),
        compiler_params=_params(1),
        name="moe_gather",
    )(pos1, pos2, zero_from, zero_count, x_pk)


def _moe_up_kernel(e_ref, c_ref, rt_ref, first_ref, x_ref, wg_ref, wu_ref, o_ref, wgb_ref, wub_ref):
    half = x_ref.shape[1]

    @pl.when(first_ref[pl.program_id(0)] == 1)
    def _():
        wgb_ref[...] = wg_ref[...].astype(BF16)
        wub_ref[...] = wu_ref[...].astype(BF16)

    lo, hi = _unpack_bf16_pairs(x_ref[...])
    gate = _dot(lo, wgb_ref[0:half, :]) + _dot(hi, wgb_ref[half:, :])
    up = _dot(lo, wub_ref[0:half, :]) + _dot(hi, wub_ref[half:, :])
    o_ref[...] = (_silu(gate) * up).astype(o_ref.dtype)


def _moe_down_kernel(e_ref, c_ref, rt_ref, first_ref, h_ref, w_ref, o_ref, wb_ref):
    @pl.when(first_ref[pl.program_id(0)] == 1)
    def _():
        wb_ref[...] = w_ref[...].astype(BF16)

    o_ref[...] = _dot(h_ref[...], wb_ref[...])


def _group_layout(counts, n_tiles):
    n_experts = counts.shape[0]
    tiles = (counts + MOE_ROW_TILE - 1) // MOE_ROW_TILE
    last = jnp.max(jnp.where(tiles > 0, jnp.arange(n_experts), 0))
    tiles = tiles.at[last].add(n_tiles - jnp.sum(tiles))
    starts = (jnp.cumsum(tiles) - tiles) * MOE_ROW_TILE
    return tiles.astype(jnp.int32), starts.astype(jnp.int32)


def _work_list(tiles, n_col_tiles, n_tiles):
    steps_end = jnp.cumsum(tiles * n_col_tiles)
    i = jnp.arange(n_tiles * n_col_tiles, dtype=jnp.int32)
    e = jnp.searchsorted(steps_end, i, side="right").astype(jnp.int32)
    k = i - (steps_end[e] - tiles[e] * n_col_tiles)
    col = k // tiles[e]
    local = k % tiles[e]
    row_tile = (jnp.cumsum(tiles) - tiles)[e] + local
    return e, col.astype(jnp.int32), row_tile.astype(jnp.int32), (local == 0).astype(jnp.int32)


def _moe_experts(xg, tiles, li, p):
    rows_g = xg.shape[0]
    d, f = p["moe_w_gate"].shape[2:]
    n_tiles = rows_g // MOE_ROW_TILE

    tf = _tile(f, 512)
    up_list = _work_list(tiles, f // tf, n_tiles)
    w_up_spec = lambda: pl.BlockSpec((None, None, d, tf), lambda i, e, c, rt, fi: (li, e[i], 0, c[i]))
    hg = pl.pallas_call(
        _moe_up_kernel,
        out_shape=jax.ShapeDtypeStruct((rows_g, f), BF16),
        grid_spec=pltpu.PrefetchScalarGridSpec(
            num_scalar_prefetch=4, grid=(n_tiles * (f // tf),),
            in_specs=[pl.BlockSpec((MOE_ROW_TILE, d // 2), lambda i, e, c, rt, fi: (rt[i], 0)),
                      w_up_spec(), w_up_spec()],
            out_specs=pl.BlockSpec((MOE_ROW_TILE, tf), lambda i, e, c, rt, fi: (rt[i], c[i])),
            scratch_shapes=[pltpu.VMEM((d, tf), BF16), pltpu.VMEM((d, tf), BF16)]),
        compiler_params=_params(1),
        name="moe_up",
    )(*up_list, xg, p["moe_w_gate"], p["moe_w_up"])

    tn = _tile(d, 2048)
    down_list = _work_list(tiles, d // tn, n_tiles)
    return pl.pallas_call(
        _moe_down_kernel,
        out_shape=jax.ShapeDtypeStruct((rows_g, d), F32),
        grid_spec=pltpu.PrefetchScalarGridSpec(
            num_scalar_prefetch=4, grid=(n_tiles * (d // tn),),
            in_specs=[pl.BlockSpec((MOE_ROW_TILE, f), lambda i, e, c, rt, fi: (rt[i], 0)),
                      pl.BlockSpec((None, None, f, tn), lambda i, e, c, rt, fi: (li, e[i], 0, c[i]))],
            out_specs=pl.BlockSpec((MOE_ROW_TILE, tn), lambda i, e, c, rt, fi: (rt[i], c[i])),
            scratch_shapes=[pltpu.VMEM((f, tn), BF16)]),
        compiler_params=_params(1),
        name="moe_down",
    )(*down_list, hg, p["moe_w_down"])


def _combine_ln_kernel(pos1_ref, pos2_ref, x_ref, wgt_ref, yg_hbm, g_ref, b_ref, of_ref, ob_ref,
                       y1_ref, y2_ref, sem_ref, *, alpha):
    tokens = x_ref.shape[0]
    base = pl.program_id(0) * tokens

    def start(t, carry):
        _row_copy(yg_hbm, pos1_ref[base + t], y1_ref, t, sem_ref).start()
        _row_copy(yg_hbm, pos2_ref[base + t], y2_ref, t, sem_ref).start()
        return carry

    def wait(t, carry):
        _row_copy(yg_hbm, 0, y1_ref, 0, sem_ref).wait()
        _row_copy(yg_hbm, 0, y2_ref, 0, sem_ref).wait()
        return carry

    lax.fori_loop(0, tokens, start, 0)
    lax.fori_loop(0, tokens, wait, 0)
    wgt = wgt_ref[...]
    h = alpha * x_ref[...] + wgt[:, 0:1] * y1_ref[...] + wgt[:, 1:2] * y2_ref[...]
    mu = jnp.mean(h, axis=-1, keepdims=True)
    dlt = h - mu
    var = jnp.mean(dlt * dlt, axis=-1, keepdims=True)
    y = dlt * lax.rsqrt(var + LN_EPS) * g_ref[...] + b_ref[...]
    of_ref[...] = y
    ob_ref[...] = y.astype(BF16)


def _moe_combine_ln(x_f, wgt, pos1, pos2, yg, g, b, alpha):
    rows, d = x_f.shape
    tokens = _tile(rows, MOE_TOKEN_TILE)
    row_spec = pl.BlockSpec((tokens, d), lambda i, p1, p2: (i, 0))
    vec_spec = pl.BlockSpec((1, d), lambda i, p1, p2: (0, 0))
    return pl.pallas_call(
        functools.partial(_combine_ln_kernel, alpha=alpha),
        out_shape=(jax.ShapeDtypeStruct((rows, d), F32), jax.ShapeDtypeStruct((rows, d), BF16)),
        grid_spec=pltpu.PrefetchScalarGridSpec(
            num_scalar_prefetch=2, grid=(rows // tokens,),
            in_specs=[row_spec, pl.BlockSpec((tokens, LANES), lambda i, p1, p2: (i, 0)),
                      pl.BlockSpec(memory_space=pl.ANY), vec_spec, vec_spec],
            out_specs=(row_spec, row_spec),
            scratch_shapes=[pltpu.VMEM((tokens, d), F32), pltpu.VMEM((tokens, d), F32), pltpu.SemaphoreType.DMA(())]),
        compiler_params=_params(1),
        name="moe_combine_ln",
    )(pos1, pos2, x_f, wgt, yg, g.reshape(1, d), b.reshape(1, d))


def _mixer(x_f, x_b, layer, p, alpha, with_packed):
    rows, d = x_f.shape
    conv_ch = p["conv_dw"].shape[2]
    d_inner = p["ssm_norm_g"].shape[1]
    xbc_dim = p["ssm_conv_w"].shape[2]
    heads = p["dt_bias"].shape[1]
    w_in_t = p["w_in_t"]
    lead = (layer,)
    tm = _tile(rows, 1024)
    c_z = 2 * conv_ch
    c_xbc = c_z + d_inner
    c_dt = c_xbc + xbc_dim
    c_gate = c_dt + heads

    tn2 = _tile(conv_ch, 256)
    u = _matmul(_mm_glu_kernel, [(x_b, 0, d)],
                [(w_in_t, _wspec_out_major(layer, 0, d, tn2), d),
                 (w_in_t, _wspec_out_major(layer, conv_ch, d, tn2), d)], [], conv_ch, F32,
                tm=tm, tn=tn2, name="in_proj_glu")
    z = _mm_out_major(_mm_plain_kernel, x_b, w_in_t, layer, c_z, d_inner, F32, tm=tm, tn=_tile(d_inner, 512),
                      name="in_proj_z")
    xbc_raw = _mm_out_major(_mm_plain_kernel, x_b, w_in_t, layer, c_xbc, xbc_dim, F32, tm=tm,
                            tn=_tile(xbc_dim, 512), name="in_proj_xbc")
    dt_raw = _mm_out_major(_mm_plain_kernel, x_b, w_in_t, layer, c_dt, LANES, F32, tm=tm, tn=LANES,
                           name="in_proj_dt")
    tng = _tile(2 * d, 512)
    gates = _mm_out_major(_mm_sigmoid_bias_kernel, x_b, w_in_t, layer, c_gate, 2 * d, F32, tm=tm, tn=tng,
                          extra=[(p["b_gate"][layer].reshape(1, 2 * d), pl.BlockSpec((1, tng), lambda n, m: (0, n)))],
                          name="in_proj_gates")

    u_act = _conv_branch(u, p["conv_dw"][layer], p["conv_ln_g"][layer], p["conv_ln_b"][layer])

    q = _tile(rows, 256)
    dt_parts = _dt_quantities(dt_raw, p["dt_bias"][layer], p["a_log"][layer], q)
    y_act = _ssd(xbc_raw, z, dt_parts, p["ssm_conv_w"][layer], p["ssm_conv_b"][layer],
                 p["d_skip"][layer], p["ssm_norm_g"][layer], q)

    tnm = _tile(d, 512)
    gate_blocks = d // tnm
    merged = _matmul(
        _mm_merge_kernel, [(u_act, 0, conv_ch), (y_act, 0, d_inner)],
        [(p["conv_pw2"], _wspec(p["conv_pw2"], lead, 0, conv_ch, tnm, 0), conv_ch),
         (p["ssm_out"], _wspec(p["ssm_out"], lead, 0, d_inner, tnm, 0), d_inner)],
        [(gates, pl.BlockSpec((tm, tnm), lambda n, m: (m, n))),
         (gates, pl.BlockSpec((tm, tnm), lambda n, m: (m, n + gate_blocks)))],
        d, BF16, tm=tm, tn=tnm, name="branch_merge")

    h = _mm_single(functools.partial(_mm_residual_kernel, alpha=alpha), merged, p["w_o"], lead, 0, d, F32,
                   tm=tm, tn=tnm, extra=[(x_f, pl.BlockSpec((tm, tnm), lambda n, m: (m, n)))], name="out_proj")
    return _layer_norm(h, p["ln_mix_g"][layer], p["ln_mix_b"][layer], with_packed)


DOWN_PROJ_MAX_K = 6144


def _down_proj(h_act, w, lead, x_f, alpha, name):
    rows, f = h_act.shape
    d = x_f.shape[1]
    splits = pl.cdiv(f, DOWN_PROJ_MAX_K)
    assert f % splits == 0
    kd = f // splits
    tm = _tile(rows, 512)
    tn = _tile(d, 512)
    res_spec = pl.BlockSpec((tm, tn), lambda n, m: (m, n))
    acc, scale = x_f, alpha
    for kb in range(splits):
        acc = _mm_single(functools.partial(_mm_residual_kernel, alpha=scale), h_act, w, lead, 0, d, F32,
                         tm=tm, tn=tn, extra=[(acc, res_spec)], name=f"{name}_{kb}", kb=kb, kd=kd)
        scale = 1.0
    return acc


def _dense_ffn(x_f, x_b, i, p, alpha):
    rows, d = x_f.shape
    f = p["ffn_w_gate"].shape[2]
    tm = _tile(rows, 1024)
    tn = _tile(f, 256)
    h_act = _matmul(_mm_swiglu_kernel, [(x_b, 0, d)],
                    [(p["ffn_w_gate"], _wspec(p["ffn_w_gate"], (i,), 0, d, tn, 0), d),
                     (p["ffn_w_up"], _wspec(p["ffn_w_up"], (i,), 0, d, tn, 0), d)], [], f, BF16,
                    tm=tm, tn=tn, name="ffn_up")
    return _down_proj(h_act, p["ffn_w_down"], (i,), x_f, alpha, "ffn_down")


def _moe_ffn_ln(x_f, x_b, x_pk, i, layer, p, alpha):
    rows = x_f.shape[0]
    n_experts = p["moe_w_gate"].shape[1]
    n_tiles = pl.cdiv(TOP_K * rows, MOE_ROW_TILE) + n_experts
    wgt, sel, counts = _router(x_b, p["moe_router"][i])
    counts = counts[0, :n_experts]
    tiles, starts = _group_layout(counts, n_tiles)
    pos1 = starts[sel[:, 0]] + sel[:, 2]
    pos2 = starts[sel[:, 1]] + sel[:, 3]
    xg = _moe_gather(x_pk, pos1, pos2, starts + counts, tiles * MOE_ROW_TILE - counts, n_tiles * MOE_ROW_TILE)
    yg = _moe_experts(xg, tiles, i, p)
    return _moe_combine_ln(x_f, wgt, pos1, pos2, yg, p["ln_ffn_g"][layer], p["ln_ffn_b"][layer], alpha)


def kernel(x, w_in, b_gate, conv_dw, conv_ln_g, conv_ln_b, conv_pw2, ssm_conv_w, ssm_conv_b, dt_bias, a_log,
           d_skip, ssm_norm_g, ssm_out, w_o, ln_mix_g, ln_mix_b, ffn_w_gate, ffn_w_up, ffn_w_down, moe_router,
           moe_w_gate, moe_w_up, moe_w_down, ln_ffn_g, ln_ffn_b):
    p = dict(w_in_t=jnp.swapaxes(w_in, 1, 2), b_gate=b_gate, conv_dw=conv_dw, conv_ln_g=conv_ln_g, conv_ln_b=conv_ln_b,
             conv_pw2=conv_pw2, ssm_conv_w=ssm_conv_w, ssm_conv_b=ssm_conv_b, dt_bias=dt_bias, a_log=a_log,
             d_skip=d_skip, ssm_norm_g=ssm_norm_g, ssm_out=ssm_out, w_o=w_o, ln_mix_g=ln_mix_g,
             ln_mix_b=ln_mix_b, ffn_w_gate=ffn_w_gate, ffn_w_up=ffn_w_up, ffn_w_down=ffn_w_down,
             moe_router=moe_router, moe_w_gate=moe_w_gate, moe_w_up=moe_w_up, moe_w_down=moe_w_down,
             ln_ffn_g=ln_ffn_g, ln_ffn_b=ln_ffn_b)
    batch = x.shape[0]
    depth = w_in.shape[0]
    alpha = (2.0 * depth) ** 0.25
    outs = []
    for b in range(batch):
        x_f = x[b]
        x_b = x_f.astype(BF16)
        for layer in range(depth):
            i = layer // 2
            moe = layer % 2 == 1
            x_f, x_b, *x_pk = _mixer(x_f, x_b, layer, p, alpha, with_packed=moe)
            if moe:
                x_f, x_b = _moe_ffn_ln(x_f, x_b, x_pk[0], i, layer, p, alpha)
            else:
                h = _dense_ffn(x_f, x_b, i, p, alpha)
                x_f, x_b = _layer_norm(h, ln_ffn_g[layer], ln_ffn_b[layer])
        outs.append(x_f)
    return jnp.stack(outs, axis=0)
```

```python
import functools

import jax
import jax.numpy as jnp
from jax import lax
from jax.experimental import pallas as pl
from jax.experimental.pallas import tpu as pltpu

F32 = jnp.float32
BF16 = jnp.bfloat16

LANES = 128
SUBLANES = 8
VMEM_LIMIT_BYTES = 56 * 1024 * 1024

SSM_GROUPS = 4
SSM_HEAD_DIM = 64
D_STATE = 128
SSM_CONV_K = 4
TOP_K = 2
LN_EPS = 1e-5
CONV_HALO = 32
SSM_HALO = 8


def _params(n_axes):
    return pltpu.CompilerParams(dimension_semantics=("arbitrary",) * n_axes,
                                vmem_limit_bytes=VMEM_LIMIT_BYTES)


def _tile(dim, pref):
    if dim <= pref:
        return dim
    t = pref
    while dim % t:
        t -= LANES
    assert t > 0, (dim, pref)
    return t


def _sigmoid(v):
    return 1.0 / (1.0 + jnp.exp(-v))


def _silu(v):
    return v * _sigmoid(v)


def _dot(a, b):
    return jnp.dot(a, b, preferred_element_type=F32)


def _cast_weight_once(w_ref, wb_ref):
    @pl.when(pl.program_id(1) == 0)
    def _():
        if len(w_ref.shape) == 3:
            wb_ref[...] = w_ref[0].T.astype(BF16)
        else:
            wb_ref[...] = w_ref[...].astype(BF16)


def _mm_plain_kernel(x_ref, w_ref, o_ref, wb_ref):
    _cast_weight_once(w_ref, wb_ref)
    o_ref[...] = _dot(x_ref[...], wb_ref[...]).astype(o_ref.dtype)


def _mm_sigmoid_bias_kernel(x_ref, w_ref, b_ref, o_ref, wb_ref):
    _cast_weight_once(w_ref, wb_ref)
    o_ref[...] = _sigmoid(_dot(x_ref[...], wb_ref[...]) + b_ref[...]).astype(o_ref.dtype)


def _mm_residual_kernel(x_ref, w_ref, r_ref, o_ref, wb_ref, *, alpha):
    _cast_weight_once(w_ref, wb_ref)
    o_ref[...] = alpha * r_ref[...] + _dot(x_ref[...], wb_ref[...])


def _mm_glu_kernel(x_ref, wa_ref, wg_ref, o_ref, wab_ref, wgb_ref):
    _cast_weight_once(wa_ref, wab_ref)
    _cast_weight_once(wg_ref, wgb_ref)
    x = x_ref[...]
    o_ref[...] = (_dot(x, wab_ref[...]) * _sigmoid(_dot(x, wgb_ref[...]))).astype(o_ref.dtype)


def _mm_swiglu_kernel(x_ref, wg_ref, wu_ref, o_ref, wgb_ref, wub_ref):
    _cast_weight_once(wg_ref, wgb_ref)
    _cast_weight_once(wu_ref, wub_ref)
    x = x_ref[...]
    o_ref[...] = (_silu(_dot(x, wgb_ref[...])) * _dot(x, wub_ref[...])).astype(o_ref.dtype)


def _mm_merge_kernel(u_ref, y_ref, w1_ref, w2_ref, gc_ref, gs_ref, o_ref, w1b_ref, w2b_ref):
    _cast_weight_once(w1_ref, w1b_ref)
    _cast_weight_once(w2_ref, w2b_ref)
    conv_branch = _dot(u_ref[...], w1b_ref[...])
    ssm_branch = _dot(y_ref[...], w2b_ref[...])
    o_ref[...] = (gc_ref[...] * conv_branch + gs_ref[...] * ssm_branch).astype(o_ref.dtype)


def _wspec(w, lead, k0, kdim, tn, col0):
    nlead = len(lead)
    assert w.ndim == nlead + 2
    return pl.BlockSpec((None,) * nlead + (kdim, tn), lambda n, m: tuple(lead) + (k0, col0 + n))


def _wspec_out_major(layer, row0, kdim, tn):
    assert row0 % SUBLANES == 0 and tn % SUBLANES == 0
    return pl.BlockSpec((pl.Element(1), pl.Element(tn), pl.Element(kdim)),
                        lambda n, m: (layer, pl.multiple_of(row0 + n * tn, SUBLANES), 0))


def _matmul(kernel, x_list, w_list, extra, out_cols, out_dtype, *, tm, tn, name):
    m_rows = x_list[0][0].shape[0]
    in_specs, operands, scratch = [], [], []
    for x, kb, kd in x_list:
        in_specs.append(pl.BlockSpec((tm, kd), lambda n, m, kb=kb: (m, kb)))
        operands.append(x)
    for w, spec, kd in w_list:
        in_specs.append(spec)
        operands.append(w)
        scratch.append(pltpu.VMEM((kd, tn), BF16))
    for a, spec in extra:
        in_specs.append(spec)
        operands.append(a)
    return pl.pallas_call(
        kernel,
        out_shape=jax.ShapeDtypeStruct((m_rows, out_cols), out_dtype),
        grid=(out_cols // tn, m_rows // tm),
        in_specs=in_specs,
        out_specs=pl.BlockSpec((tm, tn), lambda n, m: (m, n)),
        scratch_shapes=scratch,
        compiler_params=_params(2),
        name=name,
    )(*operands)


def _mm_single(kernel, x, w, lead, col_start, n_cols, out_dtype, *, tm, tn, extra=(), name, kb=0, kd=None):
    kd = x.shape[1] if kd is None else kd
    assert col_start % tn == 0 and n_cols % tn == 0
    return _matmul(kernel, [(x, kb, kd)], [(w, _wspec(w, lead, kb, kd, tn, col_start // tn), kd)], list(extra),
                   n_cols, out_dtype, tm=tm, tn=tn, name=name)


def _mm_out_major(kernel, x, wt, layer, row0, n_cols, out_dtype, *, tm, tn, extra=(), name):
    kd = x.shape[1]
    assert n_cols % tn == 0
    return _matmul(kernel, [(x, 0, kd)], [(wt, _wspec_out_major(layer, row0, kd, tn), kd)], list(extra),
                   n_cols, out_dtype, tm=tm, tn=tn, name=name)


def _pack_bf16_pairs(y):
    half = y.shape[1] // 2
    bits = lambda v: lax.bitcast_convert_type(v.astype(BF16).astype(F32), jnp.uint32)
    return (bits(y[:, :half]) >> 16) | bits(y[:, half:])


def _unpack_bf16_pairs(pk):
    lo = lax.bitcast_convert_type(pk << 16, F32).astype(BF16)
    hi = lax.bitcast_convert_type(pk & jnp.uint32(0xFFFF0000), F32).astype(BF16)
    return lo, hi


def _ln_kernel(h_ref, g_ref, b_ref, of_ref, ob_ref, *maybe_packed_ref):
    h = h_ref[...]
    mu = jnp.mean(h, axis=-1, keepdims=True)
    d = h - mu
    var = jnp.mean(d * d, axis=-1, keepdims=True)
    y = d * lax.rsqrt(var + LN_EPS) * g_ref[...] + b_ref[...]
    of_ref[...] = y
    ob_ref[...] = y.astype(BF16)
    for pk_ref in maybe_packed_ref:
        pk_ref[...] = _pack_bf16_pairs(y)


def _layer_norm(h, g, b, with_packed=False):
    rows, d = h.shape
    tr = _tile(rows, 256)
    row_spec = pl.BlockSpec((tr, d), lambda i: (i, 0))
    vec_spec = pl.BlockSpec((1, d), lambda i: (0, 0))
    out_shape = [jax.ShapeDtypeStruct((rows, d), F32), jax.ShapeDtypeStruct((rows, d), BF16)]
    out_specs = [row_spec, row_spec]
    if with_packed:
        out_shape.append(jax.ShapeDtypeStruct((rows, d // 2), jnp.uint32))
        out_specs.append(pl.BlockSpec((tr, d // 2), lambda i: (i, 0)))
    return pl.pallas_call(
        _ln_kernel,
        out_shape=tuple(out_shape),
        grid=(rows // tr,),
        in_specs=[row_spec, vec_spec, vec_spec],
        out_specs=tuple(out_specs),
        compiler_params=_params(1),
        name="layer_norm",
    )(h, g.reshape(1, d), b.reshape(1, d))


def _conv_branch_kernel(u_ref, up_ref, w_ref, g_ref, b_ref, o_ref, xin_ref, cv_ref, *, taps, sub_rows):
    rows, ch = u_ref.shape
    xin_ref[0:CONV_HALO, :] = jnp.where(pl.program_id(0) == 0, 0.0, up_ref[...])
    xin_ref[CONV_HALO:, :] = u_ref[...]

    lead = CONV_HALO - (taps - 1)
    phases = {}
    for k in range(taps):
        phases.setdefault((lead + k) % SUBLANES, []).append(k)

    def chunk(c, carry):
        lanes = pl.ds(pl.multiple_of(c * LANES, LANES), LANES)
        for r in range(rows // sub_rows):
            acc = jnp.zeros((sub_rows, LANES), F32)
            for b, ks in phases.items():
                span = (lead + ks[-1]) // SUBLANES * SUBLANES + sub_rows
                window = xin_ref[pl.ds(r * sub_rows + b, span), lanes]
                for k in ks:
                    a8 = (lead + k) // SUBLANES * SUBLANES
                    acc = acc + window[a8:a8 + sub_rows, :] * w_ref[pl.ds(k, 1), lanes]
            cv_ref[pl.ds(r * sub_rows, sub_rows), lanes] = acc
        return carry

    lax.fori_loop(0, ch // LANES, chunk, 0)
    v = cv_ref[...]
    mu = jnp.mean(v, axis=-1, keepdims=True)
    d = v - mu
    var = jnp.mean(d * d, axis=-1, keepdims=True)
    y = d * lax.rsqrt(var + LN_EPS) * g_ref[...] + b_ref[...]
    o_ref[...] = _silu(y).astype(o_ref.dtype)


def _conv_branch(u, conv_w, ln_g, ln_b):
    rows, ch = u.shape
    taps = conv_w.shape[0]
    assert taps - 1 <= CONV_HALO
    tr = _tile(rows, 256)
    halo_blocks = tr // CONV_HALO
    kernel = functools.partial(_conv_branch_kernel, taps=taps, sub_rows=_tile(tr, 64))
    vec_spec = pl.BlockSpec((1, ch), lambda i: (0, 0))
    return pl.pallas_call(
        kernel,
        out_shape=jax.ShapeDtypeStruct((rows, ch), BF16),
        grid=(rows // tr,),
        in_specs=[pl.BlockSpec((tr, ch), lambda i: (i, 0)),
                  pl.BlockSpec((CONV_HALO, ch), lambda i: (jnp.maximum(i * halo_blocks - 1, 0), 0)),
                  pl.BlockSpec((taps, ch), lambda i: (0, 0)),
                  vec_spec, vec_spec],
        out_specs=pl.BlockSpec((tr, ch), lambda i: (i, 0)),
        scratch_shapes=[pltpu.VMEM((tr + CONV_HALO, ch), F32), pltpu.VMEM((tr, ch), F32)],
        compiler_params=_params(1),
        name="conv_branch",
    )(u, u, conv_w, ln_g.reshape(1, ch), ln_b.reshape(1, ch))


def _split3(v):
    hi = v.astype(BF16)
    r1 = v - hi.astype(F32)
    mid = r1.astype(BF16)
    lo = (r1 - mid.astype(F32)).astype(BF16)
    return hi, mid, lo


def _dt_kernel(raw_ref, bias_ref, alog_ref, tri_ref, place_ref,
               cs_ref, cst_ref, dt3_ref, ecs3_ref, dte3_ref, cdec3_ref, *, heads):
    q = raw_ref.shape[0]
    lane = lax.broadcasted_iota(jnp.int32, (q, LANES), 1)
    valid = lane < heads
    pre = raw_ref[...] + bias_ref[...]
    dt = jnp.where(valid, jnp.maximum(pre, 0.0) + jnp.log(1.0 + jnp.exp(-jnp.abs(pre))), 0.0)
    d_a = dt * (-jnp.exp(alog_ref[...]))
    tri = tri_ref[...]
    cs = sum(_dot(tri, piece) for piece in _split3(d_a))
    cs = jnp.where(valid, cs, 0.0)
    cs_end = cs[q - 1:q, :]
    cs_ref[...] = cs
    cst_ref[...] = cs.T

    def stack3(v):
        return sum(_dot(piece, place_ref[j]) for j, piece in enumerate(_split3(v))).astype(BF16)

    dt3_ref[...] = stack3(dt)
    ecs3_ref[...] = stack3(jnp.exp(cs))
    dte3_ref[...] = stack3(jnp.exp(cs_end - cs))
    cdec3_ref[...] = stack3(jnp.broadcast_to(jnp.exp(cs_end), (SUBLANES, LANES)))


def _dt_quantities(dt_raw, dt_bias, a_log, q):
    rows = dt_raw.shape[0]
    heads = dt_bias.shape[0]
    assert 3 * heads <= LANES
    n_chunks = rows // q
    pad = lambda v: jnp.pad(v.astype(F32), (0, LANES - heads)).reshape(1, LANES)
    tri = (lax.broadcasted_iota(jnp.int32, (q, q), 0) >= lax.broadcasted_iota(jnp.int32, (q, q), 1)).astype(BF16)
    src = lax.broadcasted_iota(jnp.int32, (3, LANES, LANES), 1)
    dst = lax.broadcasted_iota(jnp.int32, (3, LANES, LANES), 2)
    piece = lax.broadcasted_iota(jnp.int32, (3, LANES, LANES), 0)
    place = ((src < heads) & (dst == src + piece * heads)).astype(BF16)
    row_spec = pl.BlockSpec((q, LANES), lambda i: (i, 0))
    vec_spec = pl.BlockSpec((1, LANES), lambda i: (0, 0))
    return pl.pallas_call(
        functools.partial(_dt_kernel, heads=heads),
        out_shape=(jax.ShapeDtypeStruct((rows, LANES), F32),
                   jax.ShapeDtypeStruct((LANES, rows), F32),
                   jax.ShapeDtypeStruct((rows, LANES), BF16),
                   jax.ShapeDtypeStruct((rows, LANES), BF16),
                   jax.ShapeDtypeStruct((rows, LANES), BF16),
                   jax.ShapeDtypeStruct((n_chunks, SUBLANES, LANES), BF16)),
        grid=(n_chunks,),
        in_specs=[row_spec, vec_spec, vec_spec,
                  pl.BlockSpec((q, q), lambda i: (0, 0)),
                  pl.BlockSpec((3, LANES, LANES), lambda i: (0, 0, 0))],
        out_specs=(row_spec, pl.BlockSpec((LANES, q), lambda i: (0, i)), row_spec, row_spec, row_spec,
                   pl.BlockSpec((None, SUBLANES, LANES), lambda i: (i, 0, 0))),
        compiler_params=_params(1),
        name="ssd_dt",
    )(dt_raw, pad(dt_bias), pad(a_log), tri, place)


def _ssd_kernel(xbc_ref, halo_ref, z_ref, cs_ref, cst_ref, dt3_ref, ecs3_ref, dte3_ref, cdec3_ref,
                expand_ref, cw_ref, cb_ref, dskip_ref, gnorm_ref, o_ref, xin_ref, state_ref,
                *, d_inner, groups):
    q = xbc_ref.shape[0]
    gw = d_inner // groups
    n_state = (xbc_ref.shape[1] - d_inner) // (2 * groups)
    first = pl.program_id(0) == 0

    @pl.when(first)
    def _():
        state_ref[...] = jnp.zeros_like(state_ref)

    xin_ref[0:SSM_HALO, :] = jnp.where(first, 0.0, halo_ref[...])
    xin_ref[SSM_HALO:, :] = xbc_ref[...]
    taps = cw_ref.shape[0]
    acc = jnp.broadcast_to(cb_ref[...], xbc_ref.shape)
    for k in range(taps):
        acc = acc + xin_ref[pl.ds(SSM_HALO - (taps - 1) + k, q), :] * cw_ref[pl.ds(k, 1), :]
    xbc = _silu(acc)

    expand = expand_ref[...]
    dt_x = _dot(dt3_ref[...], expand)
    ecs_x = _dot(ecs3_ref[...], expand)
    dte_x = _dot(dte3_ref[...], expand)
    cdec_x = _dot(cdec3_ref[...], expand)[0:1, :]

    xs = xbc[:, :d_inner]
    xdt = xs * dt_x
    xdt_b = xdt.astype(BF16)
    xde_b = (xdt * dte_x).astype(BF16)
    cs = cs_ref[...]
    cst = cst_ref[...]
    causal = lax.broadcasted_iota(jnp.int32, (q, q), 0) >= lax.broadcasted_iota(jnp.int32, (q, q), 1)
    low_half = lax.broadcasted_iota(jnp.int32, (q, LANES), 1) < SSM_HEAD_DIM
    heads_per_pair = LANES // SSM_HEAD_DIM

    for g in range(groups):
        cols = slice(g * gw, (g + 1) * gw)
        b_g = xbc[:, d_inner + g * n_state:d_inner + (g + 1) * n_state].astype(BF16)
        c_g = xbc[:, d_inner + (groups + g) * n_state:d_inner + (groups + g + 1) * n_state].astype(BF16)
        scores = lax.dot_general(c_g, b_g, (((1,), (1,)), ((), ())), preferred_element_type=F32)
        state = state_ref[g]
        y_off = _dot(c_g, state.astype(BF16)) * ecs_x[:, cols]
        chunk_state = lax.dot_general(b_g, xde_b[:, cols], (((0,), (0,)), ((), ())), preferred_element_type=F32)
        state_ref[g] = state * cdec_x[:, cols] + chunk_state

        pair_out = []
        for p in range(gw // LANES):
            lanes = slice(g * gw + p * LANES, g * gw + (p + 1) * LANES)
            x_pair = xdt_b[:, lanes]
            halves = []
            for e in range(heads_per_pair):
                h = (g * gw + p * LANES) // SSM_HEAD_DIM + e
                seg = cs[:, h:h + 1] - cst[h:h + 1, :]
                decay = jnp.where(causal, jnp.exp(seg), 0.0)
                halves.append(_dot((scores * decay).astype(BF16), x_pair))
            pair_out.append(jnp.where(low_half, halves[0], halves[1]))
        y_diag = jnp.concatenate(pair_out, axis=1) if len(pair_out) > 1 else pair_out[0]

        y = y_diag + y_off + dskip_ref[:, cols] * xs[:, cols]
        gated = y * _silu(z_ref[:, cols])
        inv = lax.rsqrt(jnp.mean(gated * gated, axis=-1, keepdims=True) + LN_EPS)
        o_ref[:, cols] = (gated * inv * gnorm_ref[:, cols]).astype(o_ref.dtype)


def _ssd(xbc_raw, z, dt_parts, conv_w, conv_b, d_skip, norm_g, q):
    rows, xbc_dim = xbc_raw.shape
    d_inner = z.shape[1]
    heads = d_skip.shape[0]
    n_state = (xbc_dim - d_inner) // (2 * SSM_GROUPS)
    gw = d_inner // SSM_GROUPS
    assert heads * SSM_HEAD_DIM == d_inner and gw % LANES == 0 and n_state == D_STATE
    cs, cst, dt3, ecs3, dte3, cdec3 = dt_parts
    j = lax.broadcasted_iota(jnp.int32, (LANES, d_inner), 0)
    c = lax.broadcasted_iota(jnp.int32, (LANES, d_inner), 1)
    expand = ((j < 3 * heads) & (j % heads == c // SSM_HEAD_DIM)).astype(BF16)
    dskip_x = jnp.repeat(d_skip.astype(F32), SSM_HEAD_DIM).reshape(1, d_inner)
    row = lambda w: pl.BlockSpec((q, w), lambda i: (i, 0))
    const = lambda s: pl.BlockSpec(s, lambda i: (0,) * len(s))
    halo_blocks = q // SSM_HALO
    kernel = functools.partial(_ssd_kernel, d_inner=d_inner, groups=SSM_GROUPS)
    return pl.pallas_call(
        kernel,
        out_shape=jax.ShapeDtypeStruct((rows, d_inner), BF16),
        grid=(rows // q,),
        in_specs=[row(xbc_dim),
                  pl.BlockSpec((SSM_HALO, xbc_dim), lambda i: (jnp.maximum(i * halo_blocks - 1, 0), 0)),
                  row(d_inner), row(LANES),
                  pl.BlockSpec((LANES, q), lambda i: (0, i)),
                  row(LANES), row(LANES), row(LANES),
                  pl.BlockSpec((None, SUBLANES, LANES), lambda i: (i, 0, 0)),
                  const((LANES, d_inner)), const((SSM_CONV_K, xbc_dim)), const((1, xbc_dim)),
                  const((1, d_inner)), const((1, d_inner))],
        out_specs=row(d_inner),
        scratch_shapes=[pltpu.VMEM((q + SSM_HALO, xbc_dim), F32),
                        pltpu.VMEM((SSM_GROUPS, n_state, gw), F32)],
        compiler_params=_params(1),
        name="ssd_scan",
    )(xbc_raw, xbc_raw, z, cs, cst, dt3, ecs3, dte3, cdec3, expand, conv_w,
      conv_b.reshape(1, xbc_dim), dskip_x, norm_g.reshape(1, d_inner))


MOE_ROW_TILE = 512
MOE_TOKEN_TILE = 256


def _router_kernel(x_ref, w_ref, tri_ref, wgt_ref, sel_ref, cnt_ref, carry_ref, *, n_experts):
    @pl.when(pl.program_id(0) == 0)
    def _():
        carry_ref[...] = jnp.zeros_like(carry_ref)

    logits = _dot(x_ref[...], w_ref[...].astype(BF16))
    lane_i = lax.broadcasted_iota(jnp.int32, logits.shape, 1)
    lane = lane_i.astype(F32)
    neg = -jnp.inf
    lg = jnp.where(lane_i < n_experts, logits, neg)
    m1 = jnp.max(lg, axis=-1, keepdims=True)
    i1 = jnp.min(jnp.where(lg == m1, lane, float(LANES)), axis=-1, keepdims=True)
    lg2 = jnp.where(lane == i1, neg, lg)
    m2 = jnp.max(lg2, axis=-1, keepdims=True)
    i2 = jnp.min(jnp.where(lg2 == m2, lane, float(LANES)), axis=-1, keepdims=True)
    e2 = jnp.exp(m2 - m1)
    w1 = 1.0 / (1.0 + e2)
    wgt_ref[...] = jnp.where(lane_i == 0, w1, jnp.where(lane_i == 1, e2 * w1, 0.0))

    sel1 = lane == i1
    sel2 = lane == i2
    onehot = jnp.where(sel1, 1.0, jnp.where(sel2, 1.0, 0.0))
    earlier = _dot(tri_ref[...], onehot.astype(BF16)) + carry_ref[0:1, :]
    rank1 = jnp.sum(jnp.where(sel1, earlier, 0.0), axis=-1, keepdims=True)
    rank2 = jnp.sum(jnp.where(sel2, earlier, 0.0), axis=-1, keepdims=True)
    sel = jnp.where(lane_i == 0, i1, jnp.where(lane_i == 1, i2,
                    jnp.where(lane_i == 2, rank1, jnp.where(lane_i == 3, rank2, 0.0))))
    sel_ref[...] = sel.astype(jnp.int32)
    carry_ref[...] = carry_ref[...] + jnp.sum(onehot, axis=0, keepdims=True)
    cnt_ref[...] = carry_ref[...].astype(jnp.int32)


def _router(x_bf, w_router):
    rows, d = x_bf.shape
    n_experts = w_router.shape[1]
    w_pad = jnp.pad(w_router, ((0, 0), (0, LANES - n_experts)))
    tr = _tile(rows, 512)
    tri = (lax.broadcasted_iota(jnp.int32, (tr, tr), 0) > lax.broadcasted_iota(jnp.int32, (tr, tr), 1)).astype(BF16)
    row_spec = pl.BlockSpec((tr, LANES), lambda i: (i, 0))
    return pl.pallas_call(
        functools.partial(_router_kernel, n_experts=n_experts),
        out_shape=(jax.ShapeDtypeStruct((rows, LANES), F32), jax.ShapeDtypeStruct((rows, LANES), jnp.int32),
                   jax.ShapeDtypeStruct((SUBLANES, LANES), jnp.int32)),
        grid=(rows // tr,),
        in_specs=[pl.BlockSpec((tr, d), lambda i: (i, 0)), pl.BlockSpec((d, LANES), lambda i: (0, 0)),
                  pl.BlockSpec((tr, tr), lambda i: (0, 0))],
        out_specs=(row_spec, row_spec, pl.BlockSpec((SUBLANES, LANES), lambda i: (0, 0))),
        scratch_shapes=[pltpu.VMEM((SUBLANES, LANES), F32)],
        compiler_params=_params(1),
        name="moe_router",
    )(x_bf, w_pad, tri)


def _row_copy(src, src_row, dst, dst_row, sem):
    return pltpu.make_async_copy(src.at[pl.ds(src_row, 1), :], dst.at[pl.ds(dst_row, 1), :], sem)


def _gather_kernel(pos1_ref, pos2_ref, x_hbm, o_ref, token_of_ref, sem_ref):
    tile_rows = o_ref.shape[0]
    n_tokens = pos1_ref.shape[0]

    @pl.when(pl.program_id(0) == 0)
    def _():
        def clear(r, carry):
            token_of_ref[r] = -1
            return carry

        def record(t, carry):
            token_of_ref[pos1_ref[t]] = t
            token_of_ref[pos2_ref[t]] = t
            return carry

        lax.fori_loop(0, token_of_ref.shape[0], clear, 0)
        lax.fori_loop(0, n_tokens, record, 0)

    o_ref[...] = jnp.zeros_like(o_ref)
    base = pl.program_id(0) * tile_rows

    def start(r, n_started):
        token = token_of_ref[base + r]

        @pl.when(token >= 0)
        def _():
            _row_copy(x_hbm, token, o_ref, r, sem_ref).start()

        return n_started + (token >= 0).astype(jnp.int32)

    def wait(j, carry):
        _row_copy(x_hbm, 0, o_ref, 0, sem_ref).wait()
        return carry

    lax.fori_loop(0, lax.fori_loop(0, tile_rows, start, 0), wait, 0)


def _moe_gather(x_pk, pos1, pos2, rows_g):
    rows, w = x_pk.shape
    return pl.pallas_call(
        _gather_kernel,
        out_shape=jax.ShapeDtypeStruct((rows_g, w), x_pk.dtype),
        grid_spec=pltpu.PrefetchScalarGridSpec(
            num_scalar_prefetch=2, grid=(rows_g // MOE_ROW_TILE,),
            in_specs=[pl.BlockSpec(memory_space=pl.ANY)],
            out_specs=pl.BlockSpec((MOE_ROW_TILE, w), lambda i, p1, p2: (i, 0)),
            scratch_shapes=[pltpu.SMEM((rows_g,), jnp.int32), pltpu.SemaphoreType.DMA(())]),
        compiler_params=_params(1),
        name="moe_gather",
    )(pos1, pos2, x_pk)


def _moe_up_kernel(e_ref, c_ref, rt_ref, first_ref, x_ref, wg_ref, wu_ref, o_ref, wgb_ref, wub_ref):
    half = x_ref.shape[1]

    @pl.when(first_ref[pl.program_id(0)] == 1)
    def _():
        wgb_ref[...] = wg_ref[...].astype(BF16)
        wub_ref[...] = wu_ref[...].astype(BF16)

    lo, hi = _unpack_bf16_pairs(x_ref[...])
    gate = _dot(lo, wgb_ref[0:half, :]) + _dot(hi, wgb_ref[half:, :])
    up = _dot(lo, wub_ref[0:half, :]) + _dot(hi, wub_ref[half:, :])
    o_ref[...] = (_silu(gate) * up).astype(o_ref.dtype)


def _moe_down_kernel(e_ref, c_ref, rt_ref, first_ref, h_ref, w_ref, o_ref, wb_ref):
    @pl.when(first_ref[pl.program_id(0)] == 1)
    def _():
        wb_ref[...] = w_ref[...].astype(BF16)

    o_ref[...] = _dot(h_ref[...], wb_ref[...])


def _group_layout(counts, n_tiles):
    n_experts = counts.shape[0]
    tiles = (counts + MOE_ROW_TILE - 1) // MOE_ROW_TILE
    last = jnp.max(jnp.where(tiles > 0, jnp.arange(n_experts), 0))
    tiles = tiles.at[last].add(n_tiles - jnp.sum(tiles))
    starts = (jnp.cumsum(tiles) - tiles) * MOE_ROW_TILE
    return tiles.astype(jnp.int32), starts.astype(jnp.int32)


def _work_list(tiles, n_col_tiles, n_tiles):
    steps_end = jnp.cumsum(tiles * n_col_tiles)
    i = jnp.arange(n_tiles * n_col_tiles, dtype=jnp.int32)
    e = jnp.searchsorted(steps_end, i, side="right").astype(jnp.int32)
    k = i - (steps_end[e] - tiles[e] * n_col_tiles)
    col = k // tiles[e]
    local = k % tiles[e]
    row_tile = (jnp.cumsum(tiles) - tiles)[e] + local
    return e, col.astype(jnp.int32), row_tile.astype(jnp.int32), (local == 0).astype(jnp.int32)


def _moe_experts(xg, tiles, li, p):
    rows_g = xg.shape[0]
    d, f = p["moe_w_gate"].shape[2:]
    n_tiles = rows_g // MOE_ROW_TILE

    tf = _tile(f, 512)
    up_list = _work_list(tiles, f // tf, n_tiles)
    w_up_spec = lambda: pl.BlockSpec((None, None, d, tf), lambda i, e, c, rt, fi: (li, e[i], 0, c[i]))
    hg = pl.pallas_call(
        _moe_up_kernel,
        out_shape=jax.ShapeDtypeStruct((rows_g, f), BF16),
        grid_spec=pltpu.PrefetchScalarGridSpec(
            num_scalar_prefetch=4, grid=(n_tiles * (f // tf),),
            in_specs=[pl.BlockSpec((MOE_ROW_TILE, d // 2), lambda i, e, c, rt, fi: (rt[i], 0)),
                      w_up_spec(), w_up_spec()],
            out_specs=pl.BlockSpec((MOE_ROW_TILE, tf), lambda i, e, c, rt, fi: (rt[i], c[i])),
            scratch_shapes=[pltpu.VMEM((d, tf), BF16), pltpu.VMEM((d, tf), BF16)]),
        compiler_params=_params(1),
        name="moe_up",
    )(*up_list, xg, p["moe_w_gate"], p["moe_w_up"])

    tn = _tile(d, 2048)
    down_list = _work_list(tiles, d // tn, n_tiles)
    return pl.pallas_call(
        _moe_down_kernel,
        out_shape=jax.ShapeDtypeStruct((rows_g, d), F32),
        grid_spec=pltpu.PrefetchScalarGridSpec(
            num_scalar_prefetch=4, grid=(n_tiles * (d // tn),),
            in_specs=[pl.BlockSpec((MOE_ROW_TILE, f), lambda i, e, c, rt, fi: (rt[i], 0)),
                      pl.BlockSpec((None, None, f, tn), lambda i, e, c, rt, fi: (li, e[i], 0, c[i]))],
            out_specs=pl.BlockSpec((MOE_ROW_TILE, tn), lambda i, e, c, rt, fi: (rt[i], c[i])),
            scratch_shapes=[pltpu.VMEM((f, tn), BF16)]),
        compiler_params=_params(1),
        name="moe_down",
    )(*down_list, hg, p["moe_w_down"])


def _combine_ln_kernel(pos1_ref, pos2_ref, x_ref, wgt_ref, yg_hbm, g_ref, b_ref, of_ref, ob_ref,
                       y1_ref, y2_ref, sem_ref, *, alpha):
    tokens = x_ref.shape[0]
    base = pl.program_id(0) * tokens

    def start(t, carry):
        _row_copy(yg_hbm, pos1_ref[base + t], y1_ref, t, sem_ref).start()
        _row_copy(yg_hbm, pos2_ref[base + t], y2_ref, t, sem_ref).start()
        return carry

    def wait(t, carry):
        _row_copy(yg_hbm, 0, y1_ref, 0, sem_ref).wait()
        _row_copy(yg_hbm, 0, y2_ref, 0, sem_ref).wait()
        return carry

    lax.fori_loop(0, tokens, start, 0)
    lax.fori_loop(0, tokens, wait, 0)
    wgt = wgt_ref[...]
    h = alpha * x_ref[...] + wgt[:, 0:1] * y1_ref[...] + wgt[:, 1:2] * y2_ref[...]
    mu = jnp.mean(h, axis=-1, keepdims=True)
    dlt = h - mu
    var = jnp.mean(dlt * dlt, axis=-1, keepdims=True)
    y = dlt * lax.rsqrt(var + LN_EPS) * g_ref[...] + b_ref[...]
    of_ref[...] = y
    ob_ref[...] = y.astype(BF16)


def _moe_combine_ln(x_f, wgt, pos1, pos2, yg, g, b, alpha):
    rows, d = x_f.shape
    tokens = _tile(rows, MOE_TOKEN_TILE)
    row_spec = pl.BlockSpec((tokens, d), lambda i, p1, p2: (i, 0))
    vec_spec = pl.BlockSpec((1, d), lambda i, p1, p2: (0, 0))
    return pl.pallas_call(
        functools.partial(_combine_ln_kernel, alpha=alpha),
        out_shape=(jax.ShapeDtypeStruct((rows, d), F32), jax.ShapeDtypeStruct((rows, d), BF16)),
        grid_spec=pltpu.PrefetchScalarGridSpec(
            num_scalar_prefetch=2, grid=(rows // tokens,),
            in_specs=[row_spec, pl.BlockSpec((tokens, LANES), lambda i, p1, p2: (i, 0)),
                      pl.BlockSpec(memory_space=pl.ANY), vec_spec, vec_spec],
            out_specs=(row_spec, row_spec),
            scratch_shapes=[pltpu.VMEM((tokens, d), F32), pltpu.VMEM((tokens, d), F32), pltpu.SemaphoreType.DMA(())]),
        compiler_params=_params(1),
        name="moe_combine_ln",
    )(pos1, pos2, x_f, wgt, yg, g.reshape(1, d), b.reshape(1, d))


def _mixer(x_f, x_b, layer, p, alpha, with_packed):
    rows, d = x_f.shape
    conv_ch = p["conv_dw"].shape[2]
    d_inner = p["ssm_norm_g"].shape[1]
    xbc_dim = p["ssm_conv_w"].shape[2]
    heads = p["dt_bias"].shape[1]
    w_in_t = p["w_in_t"]
    lead = (layer,)
    tm = _tile(rows, 1024)
    c_z = 2 * conv_ch
    c_xbc = c_z + d_inner
    c_dt = c_xbc + xbc_dim
    c_gate = c_dt + heads

    tn2 = _tile(conv_ch, 256)
    u = _matmul(_mm_glu_kernel, [(x_b, 0, d)],
                [(w_in_t, _wspec_out_major(layer, 0, d, tn2), d),
                 (w_in_t, _wspec_out_major(layer, conv_ch, d, tn2), d)], [], conv_ch, F32,
                tm=tm, tn=tn2, name="in_proj_glu")
    z = _mm_out_major(_mm_plain_kernel, x_b, w_in_t, layer, c_z, d_inner, F32, tm=tm, tn=_tile(d_inner, 512),
                      name="in_proj_z")
    xbc_raw = _mm_out_major(_mm_plain_kernel, x_b, w_in_t, layer, c_xbc, xbc_dim, F32, tm=tm,
                            tn=_tile(xbc_dim, 512), name="in_proj_xbc")
    dt_raw = _mm_out_major(_mm_plain_kernel, x_b, w_in_t, layer, c_dt, LANES, F32, tm=tm, tn=LANES,
                           name="in_proj_dt")
    tng = _tile(2 * d, 512)
    gates = _mm_out_major(_mm_sigmoid_bias_kernel, x_b, w_in_t, layer, c_gate, 2 * d, F32, tm=tm, tn=tng,
                          extra=[(p["b_gate"][layer].reshape(1, 2 * d), pl.BlockSpec((1, tng), lambda n, m: (0, n)))],
                          name="in_proj_gates")

    u_act = _conv_branch(u, p["conv_dw"][layer], p["conv_ln_g"][layer], p["conv_ln_b"][layer])

    q = _tile(rows, 256)
    dt_parts = _dt_quantities(dt_raw, p["dt_bias"][layer], p["a_log"][layer], q)
    y_act = _ssd(xbc_raw, z, dt_parts, p["ssm_conv_w"][layer], p["ssm_conv_b"][layer],
                 p["d_skip"][layer], p["ssm_norm_g"][layer], q)

    tnm = _tile(d, 512)
    gate_blocks = d // tnm
    merged = _matmul(
        _mm_merge_kernel, [(u_act, 0, conv_ch), (y_act, 0, d_inner)],
        [(p["conv_pw2"], _wspec(p["conv_pw2"], lead, 0, conv_ch, tnm, 0), conv_ch),
         (p["ssm_out"], _wspec(p["ssm_out"], lead, 0, d_inner, tnm, 0), d_inner)],
        [(gates, pl.BlockSpec((tm, tnm), lambda n, m: (m, n))),
         (gates, pl.BlockSpec((tm, tnm), lambda n, m: (m, n + gate_blocks)))],
        d, BF16, tm=tm, tn=tnm, name="branch_merge")

    h = _mm_single(functools.partial(_mm_residual_kernel, alpha=alpha), merged, p["w_o"], lead, 0, d, F32,
                   tm=tm, tn=tnm, extra=[(x_f, pl.BlockSpec((tm, tnm), lambda n, m: (m, n)))], name="out_proj")
    return _layer_norm(h, p["ln_mix_g"][layer], p["ln_mix_b"][layer], with_packed)


DOWN_PROJ_MAX_K = 6144


def _down_proj(h_act, w, lead, x_f, alpha, name):
    rows, f = h_act.shape
    d = x_f.shape[1]
    splits = pl.cdiv(f, DOWN_PROJ_MAX_K)
    assert f % splits == 0
    kd = f // splits
    tm = _tile(rows, 512)
    tn = _tile(d, 512)
    res_spec = pl.BlockSpec((tm, tn), lambda n, m: (m, n))
    acc, scale = x_f, alpha
    for kb in range(splits):
        acc = _mm_single(functools.partial(_mm_residual_kernel, alpha=scale), h_act, w, lead, 0, d, F32,
                         tm=tm, tn=tn, extra=[(acc, res_spec)], name=f"{name}_{kb}", kb=kb, kd=kd)
        scale = 1.0
    return acc


def _dense_ffn(x_f, x_b, i, p, alpha):
    rows, d = x_f.shape
    f = p["ffn_w_gate"].shape[2]
    tm = _tile(rows, 1024)
    tn = _tile(f, 256)
    h_act = _matmul(_mm_swiglu_kernel, [(x_b, 0, d)],
                    [(p["ffn_w_gate"], _wspec(p["ffn_w_gate"], (i,), 0, d, tn, 0), d),
                     (p["ffn_w_up"], _wspec(p["ffn_w_up"], (i,), 0, d, tn, 0), d)], [], f, BF16,
                    tm=tm, tn=tn, name="ffn_up")
    return _down_proj(h_act, p["ffn_w_down"], (i,), x_f, alpha, "ffn_down")


def _moe_ffn_ln(x_f, x_b, x_pk, i, layer, p, alpha):
    rows = x_f.shape[0]
    n_experts = p["moe_w_gate"].shape[1]
    n_tiles = pl.cdiv(TOP_K * rows, MOE_ROW_TILE) + n_experts
    wgt, sel, counts = _router(x_b, p["moe_router"][i])
    counts = counts[0, :n_experts]
    tiles, starts = _group_layout(counts, n_tiles)
    pos1 = starts[sel[:, 0]] + sel[:, 2]
    pos2 = starts[sel[:, 1]] + sel[:, 3]
    xg = _moe_gather(x_pk, pos1, pos2, n_tiles * MOE_ROW_TILE)
    yg = _moe_experts(xg, tiles, i, p)
    return _moe_combine_ln(x_f, wgt, pos1, pos2, yg, p["ln_ffn_g"][layer], p["ln_ffn_b"][layer], alpha)


def kernel(x, w_in, b_gate, conv_dw, conv_ln_g, conv_ln_b, conv_pw2, ssm_conv_w, ssm_conv_b, dt_bias, a_log,
           d_skip, ssm_norm_g, ssm_out, w_o, ln_mix_g, ln_mix_b, ffn_w_gate, ffn_w_up, ffn_w_down, moe_router,
           moe_w_gate, moe_w_up, moe_w_down, ln_ffn_g, ln_ffn_b):
    p = dict(w_in_t=jnp.swapaxes(w_in, 1, 2), b_gate=b_gate, conv_dw=conv_dw, conv_ln_g=conv_ln_g, conv_ln_b=conv_ln_b,
             conv_pw2=conv_pw2, ssm_conv_w=ssm_conv_w, ssm_conv_b=ssm_conv_b, dt_bias=dt_bias, a_log=a_log,
             d_skip=d_skip, ssm_norm_g=ssm_norm_g, ssm_out=ssm_out, w_o=w_o, ln_mix_g=ln_mix_g,
             ln_mix_b=ln_mix_b, ffn_w_gate=ffn_w_gate, ffn_w_up=ffn_w_up, ffn_w_down=ffn_w_down,
             moe_router=moe_router, moe_w_gate=moe_w_gate, moe_w_up=moe_w_up, moe_w_down=moe_w_down,
             ln_ffn_g=ln_ffn_g, ln_ffn_b=ln_ffn_b)
    batch = x.shape[0]
    depth = w_in.shape[0]
    alpha = (2.0 * depth) ** 0.25
    outs = []
    for b in range(batch):
        x_f = x[b]
        x_b = x_f.astype(BF16)
        for layer in range(depth):
            i = layer // 2
            moe = layer % 2 == 1
            x_f, x_b, *x_pk = _mixer(x_f, x_b, layer, p, alpha, with_packed=moe)
            if moe:
                x_f, x_b = _moe_ffn_ln(x_f, x_b, x_pk[0], i, layer, p, alpha)
            else:
                h = _dense_ffn(x_f, x_b, i, p, alpha)
                x_f, x_b = _layer_norm(h, ln_ffn_g[layer], ln_ffn_b[layer])
        outs.append(x_f)
    return jnp.stack(outs, axis=0)
```

```python
import functools

import jax
import jax.numpy as jnp
from jax import lax
from jax.experimental import pallas as pl
from jax.experimental.pallas import tpu as pltpu

F32 = jnp.float32
BF16 = jnp.bfloat16

LANES = 128
SUBLANES = 8
VMEM_LIMIT_BYTES = 56 * 1024 * 1024

SSM_GROUPS = 4
SSM_HEAD_DIM = 64
D_STATE = 128
SSM_CONV_K = 4
TOP_K = 2
LN_EPS = 1e-5
CONV_HALO = 32
SSM_HALO = 8


def _params(n_axes):
    return pltpu.CompilerParams(dimension_semantics=("arbitrary",) * n_axes,
                                vmem_limit_bytes=VMEM_LIMIT_BYTES)


def _tile(dim, pref):
    if dim <= pref:
        return dim
    t = pref
    while dim % t:
        t -= LANES
    assert t > 0, (dim, pref)
    return t


def _sigmoid(v):
    return 1.0 / (1.0 + jnp.exp(-v))


def _silu(v):
    return v * _sigmoid(v)


def _dot(a, b):
    return jnp.dot(a, b, preferred_element_type=F32)


def _cast_weight_once(w_ref, wb_ref):
    @pl.when(pl.program_id(1) == 0)
    def _():
        if len(w_ref.shape) == 3:
            wb_ref[...] = w_ref[0].T.astype(BF16)
        else:
            wb_ref[...] = w_ref[...].astype(BF16)


def _mm_plain_kernel(x_ref, w_ref, o_ref, wb_ref):
    _cast_weight_once(w_ref, wb_ref)
    o_ref[...] = _dot(x_ref[...], wb_ref[...]).astype(o_ref.dtype)


def _mm_sigmoid_bias_kernel(x_ref, w_ref, b_ref, o_ref, wb_ref):
    _cast_weight_once(w_ref, wb_ref)
    o_ref[...] = _sigmoid(_dot(x_ref[...], wb_ref[...]) + b_ref[...]).astype(o_ref.dtype)


def _mm_residual_kernel(x_ref, w_ref, r_ref, o_ref, wb_ref, *, alpha):
    _cast_weight_once(w_ref, wb_ref)
    o_ref[...] = alpha * r_ref[...] + _dot(x_ref[...], wb_ref[...])


def _mm_glu_kernel(x_ref, wa_ref, wg_ref, o_ref, wab_ref, wgb_ref):
    _cast_weight_once(wa_ref, wab_ref)
    _cast_weight_once(wg_ref, wgb_ref)
    x = x_ref[...]
    o_ref[...] = (_dot(x, wab_ref[...]) * _sigmoid(_dot(x, wgb_ref[...]))).astype(o_ref.dtype)


def _mm_swiglu_kernel(x_ref, wg_ref, wu_ref, o_ref, wgb_ref, wub_ref):
    _cast_weight_once(wg_ref, wgb_ref)
    _cast_weight_once(wu_ref, wub_ref)
    x = x_ref[...]
    o_ref[...] = (_silu(_dot(x, wgb_ref[...])) * _dot(x, wub_ref[...])).astype(o_ref.dtype)


def _mm_merge_kernel(u_ref, y_ref, w1_ref, w2_ref, gc_ref, gs_ref, o_ref, w1b_ref, w2b_ref):
    _cast_weight_once(w1_ref, w1b_ref)
    _cast_weight_once(w2_ref, w2b_ref)
    conv_branch = _dot(u_ref[...], w1b_ref[...])
    ssm_branch = _dot(y_ref[...], w2b_ref[...])
    o_ref[...] = (gc_ref[...] * conv_branch + gs_ref[...] * ssm_branch).astype(o_ref.dtype)


def _wspec(w, lead, k0, kdim, tn, col0):
    nlead = len(lead)
    assert w.ndim == nlead + 2
    return pl.BlockSpec((None,) * nlead + (kdim, tn), lambda n, m: tuple(lead) + (k0, col0 + n))


def _wspec_out_major(layer, row0, kdim, tn):
    assert row0 % SUBLANES == 0 and tn % SUBLANES == 0
    return pl.BlockSpec((pl.Element(1), pl.Element(tn), pl.Element(kdim)),
                        lambda n, m: (layer, pl.multiple_of(row0 + n * tn, SUBLANES), 0))


def _matmul(kernel, x_list, w_list, extra, out_cols, out_dtype, *, tm, tn, name):
    m_rows = x_list[0][0].shape[0]
    in_specs, operands, scratch = [], [], []
    for x, kb, kd in x_list:
        in_specs.append(pl.BlockSpec((tm, kd), lambda n, m, kb=kb: (m, kb)))
        operands.append(x)
    for w, spec, kd in w_list:
        in_specs.append(spec)
        operands.append(w)
        scratch.append(pltpu.VMEM((kd, tn), BF16))
    for a, spec in extra:
        in_specs.append(spec)
        operands.append(a)
    return pl.pallas_call(
        kernel,
        out_shape=jax.ShapeDtypeStruct((m_rows, out_cols), out_dtype),
        grid=(out_cols // tn, m_rows // tm),
        in_specs=in_specs,
        out_specs=pl.BlockSpec((tm, tn), lambda n, m: (m, n)),
        scratch_shapes=scratch,
        compiler_params=_params(2),
        name=name,
    )(*operands)


def _mm_single(kernel, x, w, lead, col_start, n_cols, out_dtype, *, tm, tn, extra=(), name, kb=0, kd=None):
    kd = x.shape[1] if kd is None else kd
    assert col_start % tn == 0 and n_cols % tn == 0
    return _matmul(kernel, [(x, kb, kd)], [(w, _wspec(w, lead, kb, kd, tn, col_start // tn), kd)], list(extra),
                   n_cols, out_dtype, tm=tm, tn=tn, name=name)


def _mm_out_major(kernel, x, wt, layer, row0, n_cols, out_dtype, *, tm, tn, extra=(), name):
    kd = x.shape[1]
    assert n_cols % tn == 0
    return _matmul(kernel, [(x, 0, kd)], [(wt, _wspec_out_major(layer, row0, kd, tn), kd)], list(extra),
                   n_cols, out_dtype, tm=tm, tn=tn, name=name)


def _pack_bf16_pairs(y):
    half = y.shape[1] // 2
    bits = lambda v: lax.bitcast_convert_type(v.astype(BF16).astype(F32), jnp.uint32)
    return (bits(y[:, :half]) >> 16) | bits(y[:, half:])


def _unpack_bf16_pairs(pk):
    lo = lax.bitcast_convert_type(pk << 16, F32).astype(BF16)
    hi = lax.bitcast_convert_type(pk & jnp.uint32(0xFFFF0000), F32).astype(BF16)
    return lo, hi


def _ln_kernel(h_ref, g_ref, b_ref, of_ref, ob_ref, *maybe_packed_ref):
    h = h_ref[...]
    mu = jnp.mean(h, axis=-1, keepdims=True)
    d = h - mu
    var = jnp.mean(d * d, axis=-1, keepdims=True)
    y = d * lax.rsqrt(var + LN_EPS) * g_ref[...] + b_ref[...]
    of_ref[...] = y
    ob_ref[...] = y.astype(BF16)
    for pk_ref in maybe_packed_ref:
        pk_ref[...] = _pack_bf16_pairs(y)


def _layer_norm(h, g, b, with_packed=False):
    rows, d = h.shape
    tr = _tile(rows, 256)
    row_spec = pl.BlockSpec((tr, d), lambda i: (i, 0))
    vec_spec = pl.BlockSpec((1, d), lambda i: (0, 0))
    out_shape = [jax.ShapeDtypeStruct((rows, d), F32), jax.ShapeDtypeStruct((rows, d), BF16)]
    out_specs = [row_spec, row_spec]
    if with_packed:
        out_shape.append(jax.ShapeDtypeStruct((rows, d // 2), jnp.uint32))
        out_specs.append(pl.BlockSpec((tr, d // 2), lambda i: (i, 0)))
    return pl.pallas_call(
        _ln_kernel,
        out_shape=tuple(out_shape),
        grid=(rows // tr,),
        in_specs=[row_spec, vec_spec, vec_spec],
        out_specs=tuple(out_specs),
        compiler_params=_params(1),
        name="layer_norm",
    )(h, g.reshape(1, d), b.reshape(1, d))


def _conv_branch_kernel(u_ref, up_ref, w_ref, g_ref, b_ref, o_ref, xin_ref, cv_ref, sh_ref, *, taps, sub_rows):
    rows, ch = u_ref.shape
    xin_ref[0:CONV_HALO, :] = jnp.where(pl.program_id(0) == 0, 0.0, up_ref[...])
    xin_ref[CONV_HALO:, :] = u_ref[...]

    lead = CONV_HALO - (taps - 1)
    phases = {}
    for k in range(taps):
        phases.setdefault((lead + k) % SUBLANES, []).append(k)
    shift_rows = sh_ref.shape[1]

    def chunk(c, carry):
        lanes = pl.ds(pl.multiple_of(c * LANES, LANES), LANES)
        for b in phases:
            if b > 0:
                sh_ref[b - 1] = xin_ref[pl.ds(b, shift_rows), lanes]
        for r in range(rows // sub_rows):
            acc = jnp.zeros((sub_rows, LANES), F32)
            for b, ks in phases.items():
                for k in ks:
                    start = r * sub_rows + (lead + k) // SUBLANES * SUBLANES
                    if b == 0:
                        window = xin_ref[pl.ds(start, sub_rows), lanes]
                    else:
                        window = sh_ref[b - 1, pl.ds(start, sub_rows), :]
                    acc = acc + window * w_ref[pl.ds(k, 1), lanes]
            cv_ref[pl.ds(r * sub_rows, sub_rows), lanes] = acc
        return carry

    lax.fori_loop(0, ch // LANES, chunk, 0)
    v = cv_ref[...]
    mu = jnp.mean(v, axis=-1, keepdims=True)
    d = v - mu
    var = jnp.mean(d * d, axis=-1, keepdims=True)
    y = d * lax.rsqrt(var + LN_EPS) * g_ref[...] + b_ref[...]
    o_ref[...] = _silu(y).astype(o_ref.dtype)


def _conv_branch(u, conv_w, ln_g, ln_b):
    rows, ch = u.shape
    taps = conv_w.shape[0]
    assert taps - 1 <= CONV_HALO
    tr = _tile(rows, 256)
    halo_blocks = tr // CONV_HALO
    kernel = functools.partial(_conv_branch_kernel, taps=taps, sub_rows=_tile(tr, 64))
    vec_spec = pl.BlockSpec((1, ch), lambda i: (0, 0))
    return pl.pallas_call(
        kernel,
        out_shape=jax.ShapeDtypeStruct((rows, ch), BF16),
        grid=(rows // tr,),
        in_specs=[pl.BlockSpec((tr, ch), lambda i: (i, 0)),
                  pl.BlockSpec((CONV_HALO, ch), lambda i: (jnp.maximum(i * halo_blocks - 1, 0), 0)),
                  pl.BlockSpec((taps, ch), lambda i: (0, 0)),
                  vec_spec, vec_spec],
        out_specs=pl.BlockSpec((tr, ch), lambda i: (i, 0)),
        scratch_shapes=[pltpu.VMEM((tr + CONV_HALO, ch), F32), pltpu.VMEM((tr, ch), F32),
                        pltpu.VMEM((SUBLANES - 1, tr + CONV_HALO - SUBLANES, LANES), F32)],
        compiler_params=_params(1),
        name="conv_branch",
    )(u, u, conv_w, ln_g.reshape(1, ch), ln_b.reshape(1, ch))


def _split3(v):
    hi = v.astype(BF16)
    r1 = v - hi.astype(F32)
    mid = r1.astype(BF16)
    lo = (r1 - mid.astype(F32)).astype(BF16)
    return hi, mid, lo


def _dt_kernel(raw_ref, bias_ref, alog_ref, tri_ref, place_ref,
               cs_ref, cst_ref, dt3_ref, ecs3_ref, dte3_ref, cdec3_ref, *, heads):
    q = raw_ref.shape[0]
    lane = lax.broadcasted_iota(jnp.int32, (q, LANES), 1)
    valid = lane < heads
    pre = raw_ref[...] + bias_ref[...]
    dt = jnp.where(valid, jnp.maximum(pre, 0.0) + jnp.log(1.0 + jnp.exp(-jnp.abs(pre))), 0.0)
    d_a = dt * (-jnp.exp(alog_ref[...]))
    tri = tri_ref[...]
    cs = sum(_dot(tri, piece) for piece in _split3(d_a))
    cs = jnp.where(valid, cs, 0.0)
    cs_end = cs[q - 1:q, :]
    cs_ref[...] = cs
    cst_ref[...] = cs.T

    def stack3(v):
        return sum(_dot(piece, place_ref[j]) for j, piece in enumerate(_split3(v))).astype(BF16)

    dt3_ref[...] = stack3(dt)
    ecs3_ref[...] = stack3(jnp.exp(cs))
    dte3_ref[...] = stack3(jnp.exp(cs_end - cs))
    cdec3_ref[...] = stack3(jnp.broadcast_to(jnp.exp(cs_end), (SUBLANES, LANES)))


def _dt_quantities(dt_raw, dt_bias, a_log, q):
    rows = dt_raw.shape[0]
    heads = dt_bias.shape[0]
    assert 3 * heads <= LANES
    n_chunks = rows // q
    pad = lambda v: jnp.pad(v.astype(F32), (0, LANES - heads)).reshape(1, LANES)
    tri = (lax.broadcasted_iota(jnp.int32, (q, q), 0) >= lax.broadcasted_iota(jnp.int32, (q, q), 1)).astype(BF16)
    src = lax.broadcasted_iota(jnp.int32, (3, LANES, LANES), 1)
    dst = lax.broadcasted_iota(jnp.int32, (3, LANES, LANES), 2)
    piece = lax.broadcasted_iota(jnp.int32, (3, LANES, LANES), 0)
    place = ((src < heads) & (dst == src + piece * heads)).astype(BF16)
    row_spec = pl.BlockSpec((q, LANES), lambda i: (i, 0))
    vec_spec = pl.BlockSpec((1, LANES), lambda i: (0, 0))
    return pl.pallas_call(
        functools.partial(_dt_kernel, heads=heads),
        out_shape=(jax.ShapeDtypeStruct((rows, LANES), F32),
                   jax.ShapeDtypeStruct((LANES, rows), F32),
                   jax.ShapeDtypeStruct((rows, LANES), BF16),
                   jax.ShapeDtypeStruct((rows, LANES), BF16),
                   jax.ShapeDtypeStruct((rows, LANES), BF16),
                   jax.ShapeDtypeStruct((n_chunks, SUBLANES, LANES), BF16)),
        grid=(n_chunks,),
        in_specs=[row_spec, vec_spec, vec_spec,
                  pl.BlockSpec((q, q), lambda i: (0, 0)),
                  pl.BlockSpec((3, LANES, LANES), lambda i: (0, 0, 0))],
        out_specs=(row_spec, pl.BlockSpec((LANES, q), lambda i: (0, i)), row_spec, row_spec, row_spec,
                   pl.BlockSpec((None, SUBLANES, LANES), lambda i: (i, 0, 0))),
        compiler_params=_params(1),
        name="ssd_dt",
    )(dt_raw, pad(dt_bias), pad(a_log), tri, place)


def _ssd_kernel(xbc_ref, halo_ref, z_ref, cs_ref, cst_ref, dt3_ref, ecs3_ref, dte3_ref, cdec3_ref,
                expand_ref, cw_ref, cb_ref, dskip_ref, gnorm_ref, o_ref, xin_ref, state_ref,
                *, d_inner, groups):
    q = xbc_ref.shape[0]
    gw = d_inner // groups
    n_state = (xbc_ref.shape[1] - d_inner) // (2 * groups)
    first = pl.program_id(0) == 0

    @pl.when(first)
    def _():
        state_ref[...] = jnp.zeros_like(state_ref)

    xin_ref[0:SSM_HALO, :] = jnp.where(first, 0.0, halo_ref[...])
    xin_ref[SSM_HALO:, :] = xbc_ref[...]
    taps = cw_ref.shape[0]
    acc = jnp.broadcast_to(cb_ref[...], xbc_ref.shape)
    for k in range(taps):
        acc = acc + xin_ref[pl.ds(SSM_HALO - (taps - 1) + k, q), :] * cw_ref[pl.ds(k, 1), :]
    xbc = _silu(acc)

    expand = expand_ref[...]
    dt_x = _dot(dt3_ref[...], expand)
    ecs_x = _dot(ecs3_ref[...], expand)
    dte_x = _dot(dte3_ref[...], expand)
    cdec_x = _dot(cdec3_ref[...], expand)[0:1, :]

    xs = xbc[:, :d_inner]
    xdt = xs * dt_x
    xdt_b = xdt.astype(BF16)
    xde_b = (xdt * dte_x).astype(BF16)
    cs = cs_ref[...]
    cst = cst_ref[...]
    causal = lax.broadcasted_iota(jnp.int32, (q, q), 0) >= lax.broadcasted_iota(jnp.int32, (q, q), 1)
    low_half = lax.broadcasted_iota(jnp.int32, (q, LANES), 1) < SSM_HEAD_DIM
    heads_per_pair = LANES // SSM_HEAD_DIM

    for g in range(groups):
        cols = slice(g * gw, (g + 1) * gw)
        b_g = xbc[:, d_inner + g * n_state:d_inner + (g + 1) * n_state].astype(BF16)
        c_g = xbc[:, d_inner + (groups + g) * n_state:d_inner + (groups + g + 1) * n_state].astype(BF16)
        scores = lax.dot_general(c_g, b_g, (((1,), (1,)), ((), ())), preferred_element_type=F32)
        state = state_ref[g]
        y_off = _dot(c_g, state.astype(BF16)) * ecs_x[:, cols]
        chunk_state = lax.dot_general(b_g, xde_b[:, cols], (((0,), (0,)), ((), ())), preferred_element_type=F32)
        state_ref[g] = state * cdec_x[:, cols] + chunk_state

        pair_out = []
        for p in range(gw // LANES):
            lanes = slice(g * gw + p * LANES, g * gw + (p + 1) * LANES)
            x_pair = xdt_b[:, lanes]
            halves = []
            for e in range(heads_per_pair):
                h = (g * gw + p * LANES) // SSM_HEAD_DIM + e
                seg = cs[:, h:h + 1] - cst[h:h + 1, :]
                decay = jnp.where(causal, jnp.exp(seg), 0.0)
                halves.append(_dot((scores * decay).astype(BF16), x_pair))
            pair_out.append(jnp.where(low_half, halves[0], halves[1]))
        y_diag = jnp.concatenate(pair_out, axis=1) if len(pair_out) > 1 else pair_out[0]

        y = y_diag + y_off + dskip_ref[:, cols] * xs[:, cols]
        gated = y * _silu(z_ref[:, cols])
        inv = lax.rsqrt(jnp.mean(gated * gated, axis=-1, keepdims=True) + LN_EPS)
        o_ref[:, cols] = (gated * inv * gnorm_ref[:, cols]).astype(o_ref.dtype)


def _ssd(xbc_raw, z, dt_parts, conv_w, conv_b, d_skip, norm_g, q):
    rows, xbc_dim = xbc_raw.shape
    d_inner = z.shape[1]
    heads = d_skip.shape[0]
    n_state = (xbc_dim - d_inner) // (2 * SSM_GROUPS)
    gw = d_inner // SSM_GROUPS
    assert heads * SSM_HEAD_DIM == d_inner and gw % LANES == 0 and n_state == D_STATE
    cs, cst, dt3, ecs3, dte3, cdec3 = dt_parts
    j = lax.broadcasted_iota(jnp.int32, (LANES, d_inner), 0)
    c = lax.broadcasted_iota(jnp.int32, (LANES, d_inner), 1)
    expand = ((j < 3 * heads) & (j % heads == c // SSM_HEAD_DIM)).astype(BF16)
    dskip_x = jnp.repeat(d_skip.astype(F32), SSM_HEAD_DIM).reshape(1, d_inner)
    row = lambda w: pl.BlockSpec((q, w), lambda i: (i, 0))
    const = lambda s: pl.BlockSpec(s, lambda i: (0,) * len(s))
    halo_blocks = q // SSM_HALO
    kernel = functools.partial(_ssd_kernel, d_inner=d_inner, groups=SSM_GROUPS)
    return pl.pallas_call(
        kernel,
        out_shape=jax.ShapeDtypeStruct((rows, d_inner), BF16),
        grid=(rows // q,),
        in_specs=[row(xbc_dim),
                  pl.BlockSpec((SSM_HALO, xbc_dim), lambda i: (jnp.maximum(i * halo_blocks - 1, 0), 0)),
                  row(d_inner), row(LANES),
                  pl.BlockSpec((LANES, q), lambda i: (0, i)),
                  row(LANES), row(LANES), row(LANES),
                  pl.BlockSpec((None, SUBLANES, LANES), lambda i: (i, 0, 0)),
                  const((LANES, d_inner)), const((SSM_CONV_K, xbc_dim)), const((1, xbc_dim)),
                  const((1, d_inner)), const((1, d_inner))],
        out_specs=row(d_inner),
        scratch_shapes=[pltpu.VMEM((q + SSM_HALO, xbc_dim), F32),
                        pltpu.VMEM((SSM_GROUPS, n_state, gw), F32)],
        compiler_params=_params(1),
        name="ssd_scan",
    )(xbc_raw, xbc_raw, z, cs, cst, dt3, ecs3, dte3, cdec3, expand, conv_w,
      conv_b.reshape(1, xbc_dim), dskip_x, norm_g.reshape(1, d_inner))


MOE_ROW_TILE = 512
MOE_TOKEN_TILE = 256
DMA_LOOP_UNROLL = 8


def _router_kernel(x_ref, w_ref, tri_ref, wgt_ref, sel_ref, cnt_ref, carry_ref, *, n_experts):
    @pl.when(pl.program_id(0) == 0)
    def _():
        carry_ref[...] = jnp.zeros_like(carry_ref)

    logits = _dot(x_ref[...], w_ref[...].astype(BF16))
    lane_i = lax.broadcasted_iota(jnp.int32, logits.shape, 1)
    lane = lane_i.astype(F32)
    neg = -jnp.inf
    lg = jnp.where(lane_i < n_experts, logits, neg)
    m1 = jnp.max(lg, axis=-1, keepdims=True)
    i1 = jnp.min(jnp.where(lg == m1, lane, float(LANES)), axis=-1, keepdims=True)
    lg2 = jnp.where(lane == i1, neg, lg)
    m2 = jnp.max(lg2, axis=-1, keepdims=True)
    i2 = jnp.min(jnp.where(lg2 == m2, lane, float(LANES)), axis=-1, keepdims=True)
    e2 = jnp.exp(m2 - m1)
    w1 = 1.0 / (1.0 + e2)
    wgt_ref[...] = jnp.where(lane_i == 0, w1, jnp.where(lane_i == 1, e2 * w1, 0.0))

    sel1 = lane == i1
    sel2 = lane == i2
    onehot = jnp.where(sel1, 1.0, jnp.where(sel2, 1.0, 0.0))
    earlier = _dot(tri_ref[...], onehot.astype(BF16)) + carry_ref[0:1, :]
    rank1 = jnp.sum(jnp.where(sel1, earlier, 0.0), axis=-1, keepdims=True)
    rank2 = jnp.sum(jnp.where(sel2, earlier, 0.0), axis=-1, keepdims=True)
    sel = jnp.where(lane_i == 0, i1, jnp.where(lane_i == 1, i2,
                    jnp.where(lane_i == 2, rank1, jnp.where(lane_i == 3, rank2, 0.0))))
    sel_ref[...] = sel.astype(jnp.int32)
    carry_ref[...] = carry_ref[...] + jnp.sum(onehot, axis=0, keepdims=True)
    cnt_ref[...] = carry_ref[...].astype(jnp.int32)


def _router(x_bf, w_router):
    rows, d = x_bf.shape
    n_experts = w_router.shape[1]
    w_pad = jnp.pad(w_router, ((0, 0), (0, LANES - n_experts)))
    tr = _tile(rows, 512)
    tri = (lax.broadcasted_iota(jnp.int32, (tr, tr), 0) > lax.broadcasted_iota(jnp.int32, (tr, tr), 1)).astype(BF16)
    row_spec = pl.BlockSpec((tr, LANES), lambda i: (i, 0))
    return pl.pallas_call(
        functools.partial(_router_kernel, n_experts=n_experts),
        out_shape=(jax.ShapeDtypeStruct((rows, LANES), F32), jax.ShapeDtypeStruct((rows, LANES), jnp.int32),
                   jax.ShapeDtypeStruct((SUBLANES, LANES), jnp.int32)),
        grid=(rows // tr,),
        in_specs=[pl.BlockSpec((tr, d), lambda i: (i, 0)), pl.BlockSpec((d, LANES), lambda i: (0, 0)),
                  pl.BlockSpec((tr, tr), lambda i: (0, 0))],
        out_specs=(row_spec, row_spec, pl.BlockSpec((SUBLANES, LANES), lambda i: (0, 0))),
        scratch_shapes=[pltpu.VMEM((SUBLANES, LANES), F32)],
        compiler_params=_params(1),
        name="moe_router",
    )(x_bf, w_pad, tri)


def _row_copy(src, src_row, dst, dst_row, sem):
    return pltpu.make_async_copy(src.at[pl.ds(src_row, 1), :], dst.at[pl.ds(dst_row, 1), :], sem)


def _gather_kernel(pos1_ref, pos2_ref, x_hbm, o_ref, token_of_ref, sem_ref):
    tile_rows = o_ref.shape[0]
    n_tokens = pos1_ref.shape[0]

    @pl.when(pl.program_id(0) == 0)
    def _():
        def clear(r, carry):
            token_of_ref[r] = -1
            return carry

        def record(t, carry):
            token_of_ref[pos1_ref[t]] = t
            token_of_ref[pos2_ref[t]] = t
            return carry

        lax.fori_loop(0, token_of_ref.shape[0], clear, 0, unroll=DMA_LOOP_UNROLL)
        lax.fori_loop(0, n_tokens, record, 0, unroll=DMA_LOOP_UNROLL)

    o_ref[...] = jnp.zeros_like(o_ref)
    base = pl.program_id(0) * tile_rows

    def start(r, n_started):
        token = token_of_ref[base + r]

        @pl.when(token >= 0)
        def _():
            _row_copy(x_hbm, token, o_ref, r, sem_ref).start()

        return n_started + (token >= 0).astype(jnp.int32)

    def wait(j, carry):
        _row_copy(x_hbm, 0, o_ref, 0, sem_ref).wait()
        return carry

    n_started = lax.fori_loop(0, tile_rows, start, 0, unroll=DMA_LOOP_UNROLL)
    lax.fori_loop(0, n_started, wait, 0)


def _moe_gather(x_pk, pos1, pos2, rows_g):
    rows, w = x_pk.shape
    return pl.pallas_call(
        _gather_kernel,
        out_shape=jax.ShapeDtypeStruct((rows_g, w), x_pk.dtype),
        grid_spec=pltpu.PrefetchScalarGridSpec(
            num_scalar_prefetch=2, grid=(rows_g // MOE_ROW_TILE,),
            in_specs=[pl.BlockSpec(memory_space=pl.ANY)],
            out_specs=pl.BlockSpec((MOE_ROW_TILE, w), lambda i, p1, p2: (i, 0)),
            scratch_shapes=[pltpu.SMEM((rows_g,), jnp.int32), pltpu.SemaphoreType.DMA(())]),
        compiler_params=_params(1),
        name="moe_gather",
    )(pos1, pos2, x_pk)


def _moe_up_kernel(e_ref, c_ref, rt_ref, first_ref, live_ref, x_ref, wg_ref, wu_ref, o_ref, wgb_ref, wub_ref):
    half = x_ref.shape[1]
    step = pl.program_id(0)

    @pl.when(first_ref[step] == 1)
    def _():
        wgb_ref[...] = wg_ref[...].astype(BF16)
        wub_ref[...] = wu_ref[...].astype(BF16)

    @pl.when(live_ref[step] == 1)
    def _():
        lo, hi = _unpack_bf16_pairs(x_ref[...])
        gate = _dot(lo, wgb_ref[0:half, :]) + _dot(hi, wgb_ref[half:, :])
        up = _dot(lo, wub_ref[0:half, :]) + _dot(hi, wub_ref[half:, :])
        o_ref[...] = (_silu(gate) * up).astype(o_ref.dtype)

    @pl.when(live_ref[step] == 0)
    def _():
        o_ref[...] = jnp.zeros_like(o_ref)


def _moe_down_kernel(e_ref, c_ref, rt_ref, first_ref, live_ref, h_ref, w_ref, o_ref, wb_ref):
    step = pl.program_id(0)

    @pl.when(first_ref[step] == 1)
    def _():
        wb_ref[...] = w_ref[...].astype(BF16)

    @pl.when(live_ref[step] == 1)
    def _():
        o_ref[...] = _dot(h_ref[...], wb_ref[...])

    @pl.when(live_ref[step] == 0)
    def _():
        o_ref[...] = jnp.zeros_like(o_ref)


def _group_layout(counts, n_tiles):
    n_experts = counts.shape[0]
    tiles = (counts + MOE_ROW_TILE - 1) // MOE_ROW_TILE
    last = jnp.max(jnp.where(tiles > 0, jnp.arange(n_experts), 0))
    tiles = tiles.at[last].add(n_tiles - jnp.sum(tiles))
    starts = (jnp.cumsum(tiles) - tiles) * MOE_ROW_TILE
    return tiles.astype(jnp.int32), starts.astype(jnp.int32)


def _work_list(tiles, counts, n_col_tiles, n_tiles):
    steps_end = jnp.cumsum(tiles * n_col_tiles)
    i = jnp.arange(n_tiles * n_col_tiles, dtype=jnp.int32)
    e = jnp.sum((i[:, None] >= steps_end[None, :]).astype(jnp.int32), axis=1)
    k = i - (steps_end[e] - tiles[e] * n_col_tiles)
    col = k // tiles[e]
    local = k % tiles[e]
    row_tile = (jnp.cumsum(tiles) - tiles)[e] + local
    live = local * MOE_ROW_TILE < counts[e]
    return (e, col.astype(jnp.int32), row_tile.astype(jnp.int32), (local == 0).astype(jnp.int32),
            live.astype(jnp.int32))


def _moe_experts(xg, tiles, counts, li, p):
    rows_g = xg.shape[0]
    d, f = p["moe_w_gate"].shape[2:]
    n_tiles = rows_g // MOE_ROW_TILE

    tf = _tile(f, 512)
    up_list = _work_list(tiles, counts, f // tf, n_tiles)
    w_up_spec = lambda: pl.BlockSpec((None, None, d, tf), lambda i, e, c, rt, fi, lv: (li, e[i], 0, c[i]))
    hg = pl.pallas_call(
        _moe_up_kernel,
        out_shape=jax.ShapeDtypeStruct((rows_g, f), BF16),
        grid_spec=pltpu.PrefetchScalarGridSpec(
            num_scalar_prefetch=5, grid=(n_tiles * (f // tf),),
            in_specs=[pl.BlockSpec((MOE_ROW_TILE, d // 2), lambda i, e, c, rt, fi, lv: (rt[i], 0)),
                      w_up_spec(), w_up_spec()],
            out_specs=pl.BlockSpec((MOE_ROW_TILE, tf), lambda i, e, c, rt, fi, lv: (rt[i], c[i])),
            scratch_shapes=[pltpu.VMEM((d, tf), BF16), pltpu.VMEM((d, tf), BF16)]),
        compiler_params=_params(1),
        name="moe_up",
    )(*up_list, xg, p["moe_w_gate"], p["moe_w_up"])

    tn = _tile(d, 2048)
    down_list = _work_list(tiles, counts, d // tn, n_tiles)
    return pl.pallas_call(
        _moe_down_kernel,
        out_shape=jax.ShapeDtypeStruct((rows_g, d), F32),
        grid_spec=pltpu.PrefetchScalarGridSpec(
            num_scalar_prefetch=5, grid=(n_tiles * (d // tn),),
            in_specs=[pl.BlockSpec((MOE_ROW_TILE, f), lambda i, e, c, rt, fi, lv: (rt[i], 0)),
                      pl.BlockSpec((None, None, f, tn), lambda i, e, c, rt, fi, lv: (li, e[i], 0, c[i]))],
            out_specs=pl.BlockSpec((MOE_ROW_TILE, tn), lambda i, e, c, rt, fi, lv: (rt[i], c[i])),
            scratch_shapes=[pltpu.VMEM((f, tn), BF16)]),
        compiler_params=_params(1),
        name="moe_down",
    )(*down_list, hg, p["moe_w_down"])


def _combine_ln_kernel(pos1_ref, pos2_ref, x_ref, wgt_ref, yg_hbm, g_ref, b_ref, of_ref, ob_ref,
                       y1_ref, y2_ref, sem_ref, *, alpha):
    tokens = x_ref.shape[0]
    base = pl.program_id(0) * tokens

    def start(t, carry):
        _row_copy(yg_hbm, pos1_ref[base + t], y1_ref, t, sem_ref).start()
        _row_copy(yg_hbm, pos2_ref[base + t], y2_ref, t, sem_ref).start()
        return carry

    def wait(t, carry):
        _row_copy(yg_hbm, 0, y1_ref, 0, sem_ref).wait()
        _row_copy(yg_hbm, 0, y2_ref, 0, sem_ref).wait()
        return carry

    lax.fori_loop(0, tokens, start, 0, unroll=DMA_LOOP_UNROLL)
    lax.fori_loop(0, tokens, wait, 0, unroll=DMA_LOOP_UNROLL)
    wgt = wgt_ref[...]
    h = alpha * x_ref[...] + wgt[:, 0:1] * y1_ref[...] + wgt[:, 1:2] * y2_ref[...]
    mu = jnp.mean(h, axis=-1, keepdims=True)
    dlt = h - mu
    var = jnp.mean(dlt * dlt, axis=-1, keepdims=True)
    y = dlt * lax.rsqrt(var + LN_EPS) * g_ref[...] + b_ref[...]
    of_ref[...] = y
    ob_ref[...] = y.astype(BF16)


def _moe_combine_ln(x_f, wgt, pos1, pos2, yg, g, b, alpha):
    rows, d = x_f.shape
    tokens = _tile(rows, MOE_TOKEN_TILE)
    row_spec = pl.BlockSpec((tokens, d), lambda i, p1, p2: (i, 0))
    vec_spec = pl.BlockSpec((1, d), lambda i, p1, p2: (0, 0))
    return pl.pallas_call(
        functools.partial(_combine_ln_kernel, alpha=alpha),
        out_shape=(jax.ShapeDtypeStruct((rows, d), F32), jax.ShapeDtypeStruct((rows, d), BF16)),
        grid_spec=pltpu.PrefetchScalarGridSpec(
            num_scalar_prefetch=2, grid=(rows // tokens,),
            in_specs=[row_spec, pl.BlockSpec((tokens, LANES), lambda i, p1, p2: (i, 0)),
                      pl.BlockSpec(memory_space=pl.ANY), vec_spec, vec_spec],
            out_specs=(row_spec, row_spec),
            scratch_shapes=[pltpu.VMEM((tokens, d), F32), pltpu.VMEM((tokens, d), F32), pltpu.SemaphoreType.DMA(())]),
        compiler_params=_params(1),
        name="moe_combine_ln",
    )(pos1, pos2, x_f, wgt, yg, g.reshape(1, d), b.reshape(1, d))


def _mixer(x_f, x_b, layer, p, alpha, with_packed):
    rows, d = x_f.shape
    conv_ch = p["conv_dw"].shape[2]
    d_inner = p["ssm_norm_g"].shape[1]
    xbc_dim = p["ssm_conv_w"].shape[2]
    heads = p["dt_bias"].shape[1]
    w_in_t = p["w_in_t"]
    lead = (layer,)
    tm = _tile(rows, 1024)
    c_z = 2 * conv_ch
    c_xbc = c_z + d_inner
    c_dt = c_xbc + xbc_dim
    c_gate = c_dt + heads

    tn2 = _tile(conv_ch, 256)
    u = _matmul(_mm_glu_kernel, [(x_b, 0, d)],
                [(w_in_t, _wspec_out_major(layer, 0, d, tn2), d),
                 (w_in_t, _wspec_out_major(layer, conv_ch, d, tn2), d)], [], conv_ch, F32,
                tm=tm, tn=tn2, name="in_proj_glu")
    z = _mm_out_major(_mm_plain_kernel, x_b, w_in_t, layer, c_z, d_inner, F32, tm=tm, tn=_tile(d_inner, 512),
                      name="in_proj_z")
    xbc_raw = _mm_out_major(_mm_plain_kernel, x_b, w_in_t, layer, c_xbc, xbc_dim, F32, tm=tm,
                            tn=_tile(xbc_dim, 512), name="in_proj_xbc")
    dt_raw = _mm_out_major(_mm_plain_kernel, x_b, w_in_t, layer, c_dt, LANES, F32, tm=tm, tn=LANES,
                           name="in_proj_dt")
    tng = _tile(2 * d, 512)
    gates = _mm_out_major(_mm_sigmoid_bias_kernel, x_b, w_in_t, layer, c_gate, 2 * d, F32, tm=tm, tn=tng,
                          extra=[(p["b_gate"][layer].reshape(1, 2 * d), pl.BlockSpec((1, tng), lambda n, m: (0, n)))],
                          name="in_proj_gates")

    u_act = _conv_branch(u, p["conv_dw"][layer], p["conv_ln_g"][layer], p["conv_ln_b"][layer])

    q = _tile(rows, 256)
    dt_parts = _dt_quantities(dt_raw, p["dt_bias"][layer], p["a_log"][layer], q)
    y_act = _ssd(xbc_raw, z, dt_parts, p["ssm_conv_w"][layer], p["ssm_conv_b"][layer],
                 p["d_skip"][layer], p["ssm_norm_g"][layer], q)

    tnm = _tile(d, 512)
    gate_blocks = d // tnm
    merged = _matmul(
        _mm_merge_kernel, [(u_act, 0, conv_ch), (y_act, 0, d_inner)],
        [(p["conv_pw2"], _wspec(p["conv_pw2"], lead, 0, conv_ch, tnm, 0), conv_ch),
         (p["ssm_out"], _wspec(p["ssm_out"], lead, 0, d_inner, tnm, 0), d_inner)],
        [(gates, pl.BlockSpec((tm, tnm), lambda n, m: (m, n))),
         (gates, pl.BlockSpec((tm, tnm), lambda n, m: (m, n + gate_blocks)))],
        d, BF16, tm=tm, tn=tnm, name="branch_merge")

    h = _mm_single(functools.partial(_mm_residual_kernel, alpha=alpha), merged, p["w_o"], lead, 0, d, F32,
                   tm=tm, tn=tnm, extra=[(x_f, pl.BlockSpec((tm, tnm), lambda n, m: (m, n)))], name="out_proj")
    return _layer_norm(h, p["ln_mix_g"][layer], p["ln_mix_b"][layer], with_packed)


DOWN_PROJ_MAX_K = 6144


def _down_proj(h_act, w, lead, x_f, alpha, name):
    rows, f = h_act.shape
    d = x_f.shape[1]
    splits = pl.cdiv(f, DOWN_PROJ_MAX_K)
    assert f % splits == 0
    kd = f // splits
    tm = _tile(rows, 512)
    tn = _tile(d, 512)
    res_spec = pl.BlockSpec((tm, tn), lambda n, m: (m, n))
    acc, scale = x_f, alpha
    for kb in range(splits):
        acc = _mm_single(functools.partial(_mm_residual_kernel, alpha=scale), h_act, w, lead, 0, d, F32,
                         tm=tm, tn=tn, extra=[(acc, res_spec)], name=f"{name}_{kb}", kb=kb, kd=kd)
        scale = 1.0
    return acc


def _dense_ffn(x_f, x_b, i, p, alpha):
    rows, d = x_f.shape
    f = p["ffn_w_gate"].shape[2]
    tm = _tile(rows, 1024)
    tn = _tile(f, 256)
    h_act = _matmul(_mm_swiglu_kernel, [(x_b, 0, d)],
                    [(p["ffn_w_gate"], _wspec(p["ffn_w_gate"], (i,), 0, d, tn, 0), d),
                     (p["ffn_w_up"], _wspec(p["ffn_w_up"], (i,), 0, d, tn, 0), d)], [], f, BF16,
                    tm=tm, tn=tn, name="ffn_up")
    return _down_proj(h_act, p["ffn_w_down"], (i,), x_f, alpha, "ffn_down")


def _moe_ffn_ln(x_f, x_b, x_pk, i, layer, p, alpha):
    rows = x_f.shape[0]
    n_experts = p["moe_w_gate"].shape[1]
    n_tiles = pl.cdiv(TOP_K * rows, MOE_ROW_TILE) + n_experts
    wgt, sel, counts = _router(x_b, p["moe_router"][i])
    counts = counts[0, :n_experts]
    tiles, starts = _group_layout(counts, n_tiles)
    pos1 = starts[sel[:, 0]] + sel[:, 2]
    pos2 = starts[sel[:, 1]] + sel[:, 3]
    xg = _moe_gather(x_pk, pos1, pos2, n_tiles * MOE_ROW_TILE)
    yg = _moe_experts(xg, tiles, counts, i, p)
    return _moe_combine_ln(x_f, wgt, pos1, pos2, yg, p["ln_ffn_g"][layer], p["ln_ffn_b"][layer], alpha)


def kernel(x, w_in, b_gate, conv_dw, conv_ln_g, conv_ln_b, conv_pw2, ssm_conv_w, ssm_conv_b, dt_bias, a_log,
           d_skip, ssm_norm_g, ssm_out, w_o, ln_mix_g, ln_mix_b, ffn_w_gate, ffn_w_up, ffn_w_down, moe_router,
           moe_w_gate, moe_w_up, moe_w_down, ln_ffn_g, ln_ffn_b):
    p = dict(w_in_t=jnp.swapaxes(w_in, 1, 2), b_gate=b_gate, conv_dw=conv_dw, conv_ln_g=conv_ln_g, conv_ln_b=conv_ln_b,
             conv_pw2=conv_pw2, ssm_conv_w=ssm_conv_w, ssm_conv_b=ssm_conv_b, dt_bias=dt_bias, a_log=a_log,
             d_skip=d_skip, ssm_norm_g=ssm_norm_g, ssm_out=ssm_out, w_o=w_o, ln_mix_g=ln_mix_g,
             ln_mix_b=ln_mix_b, ffn_w_gate=ffn_w_gate, ffn_w_up=ffn_w_up, ffn_w_down=ffn_w_down,
             moe_router=moe_router, moe_w_gate=moe_w_gate, moe_w_up=moe_w_up, moe_w_down=moe_w_down,
             ln_ffn_g=ln_ffn_g, ln_ffn_b=ln_ffn_b)
    batch = x.shape[0]
    depth = w_in.shape[0]
    alpha = (2.0 * depth) ** 0.25
    outs = []
    for b in range(batch):
        x_f = x[b]
        x_b = x_f.astype(BF16)
        for layer in range(depth):
            i = layer // 2
            moe = layer % 2 == 1
            x_f, x_b, *x_pk = _mixer(x_f, x_b, layer, p, alpha, with_packed=moe)
            if moe:
                x_f, x_b = _moe_ffn_ln(x_f, x_b, x_pk[0], i, layer, p, alpha)
            else:
                h = _dense_ffn(x_f, x_b, i, p, alpha)
                x_f, x_b = _layer_norm(h, ln_ffn_g[layer], ln_ffn_b[layer])
        outs.append(x_f)
    return jnp.stack(outs, axis=0)
```

```python
import functools

import jax
import jax.numpy as jnp
from jax import lax
from jax.experimental import pallas as pl
from jax.experimental.pallas import tpu as pltpu

F32 = jnp.float32
BF16 = jnp.bfloat16

LANES = 128
SUBLANES = 8
VMEM_LIMIT_BYTES = 56 * 1024 * 1024

SSM_GROUPS = 4
SSM_HEAD_DIM = 64
D_STATE = 128
SSM_CONV_K = 4
TOP_K = 2
LN_EPS = 1e-5
CONV_HALO = 32
SSM_HALO = 8


def _params(n_axes):
    return pltpu.CompilerParams(dimension_semantics=("arbitrary",) * n_axes,
                                vmem_limit_bytes=VMEM_LIMIT_BYTES)


def _tile(dim, pref):
    if dim <= pref:
        return dim
    t = pref
    while dim % t:
        t -= LANES
    assert t > 0, (dim, pref)
    return t


def _sigmoid(v):
    return 1.0 / (1.0 + jnp.exp(-v))


def _silu(v):
    return v * _sigmoid(v)


def _dot(a, b):
    return jnp.dot(a, b, preferred_element_type=F32)


def _cast_weight_once(w_ref, wb_ref):
    @pl.when(pl.program_id(1) == 0)
    def _():
        if len(w_ref.shape) == 3:
            wb_ref[...] = w_ref[0].T.astype(BF16)
        else:
            wb_ref[...] = w_ref[...].astype(BF16)


def _mm_plain_kernel(x_ref, w_ref, o_ref, wb_ref):
    _cast_weight_once(w_ref, wb_ref)
    o_ref[...] = _dot(x_ref[...], wb_ref[...]).astype(o_ref.dtype)


def _mm_sigmoid_bias_kernel(x_ref, w_ref, b_ref, o_ref, wb_ref):
    _cast_weight_once(w_ref, wb_ref)
    o_ref[...] = _sigmoid(_dot(x_ref[...], wb_ref[...]) + b_ref[...]).astype(o_ref.dtype)


def _mm_residual_kernel(x_ref, w_ref, r_ref, o_ref, wb_ref, *, alpha):
    _cast_weight_once(w_ref, wb_ref)
    o_ref[...] = alpha * r_ref[...] + _dot(x_ref[...], wb_ref[...])


def _mm_glu_kernel(x_ref, wa_ref, wg_ref, o_ref, wab_ref, wgb_ref):
    _cast_weight_once(wa_ref, wab_ref)
    _cast_weight_once(wg_ref, wgb_ref)
    x = x_ref[...]
    o_ref[...] = (_dot(x, wab_ref[...]) * _sigmoid(_dot(x, wgb_ref[...]))).astype(o_ref.dtype)


def _mm_swiglu_kernel(x_ref, wg_ref, wu_ref, o_ref, wgb_ref, wub_ref):
    _cast_weight_once(wg_ref, wgb_ref)
    _cast_weight_once(wu_ref, wub_ref)
    x = x_ref[...]
    o_ref[...] = (_silu(_dot(x, wgb_ref[...])) * _dot(x, wub_ref[...])).astype(o_ref.dtype)


def _mm_merge_kernel(u_ref, y_ref, w1_ref, w2_ref, gc_ref, gs_ref, o_ref, w1b_ref, w2b_ref):
    _cast_weight_once(w1_ref, w1b_ref)
    _cast_weight_once(w2_ref, w2b_ref)
    conv_branch = _dot(u_ref[...], w1b_ref[...])
    ssm_branch = _dot(y_ref[...], w2b_ref[...])
    o_ref[...] = (gc_ref[...] * conv_branch + gs_ref[...] * ssm_branch).astype(o_ref.dtype)


def _wspec(w, lead, k0, kdim, tn, col0):
    nlead = len(lead)
    assert w.ndim == nlead + 2
    return pl.BlockSpec((None,) * nlead + (kdim, tn), lambda n, m: tuple(lead) + (k0, col0 + n))


def _wspec_out_major(layer, row0, kdim, tn):
    assert row0 % SUBLANES == 0 and tn % SUBLANES == 0
    return pl.BlockSpec((pl.Element(1), pl.Element(tn), pl.Element(kdim)),
                        lambda n, m: (layer, pl.multiple_of(row0 + n * tn, SUBLANES), 0))


def _matmul(kernel, x_list, w_list, extra, out_cols, out_dtype, *, tm, tn, name):
    m_rows = x_list[0][0].shape[0]
    in_specs, operands, scratch = [], [], []
    for x, kb, kd in x_list:
        in_specs.append(pl.BlockSpec((tm, kd), lambda n, m, kb=kb: (m, kb)))
        operands.append(x)
    for w, spec, kd in w_list:
        in_specs.append(spec)
        operands.append(w)
        scratch.append(pltpu.VMEM((kd, tn), BF16))
    for a, spec in extra:
        in_specs.append(spec)
        operands.append(a)
    return pl.pallas_call(
        kernel,
        out_shape=jax.ShapeDtypeStruct((m_rows, out_cols), out_dtype),
        grid=(out_cols // tn, m_rows // tm),
        in_specs=in_specs,
        out_specs=pl.BlockSpec((tm, tn), lambda n, m: (m, n)),
        scratch_shapes=scratch,
        compiler_params=_params(2),
        name=name,
    )(*operands)


def _mm_single(kernel, x, w, lead, col_start, n_cols, out_dtype, *, tm, tn, extra=(), name, kb=0, kd=None):
    kd = x.shape[1] if kd is None else kd
    assert col_start % tn == 0 and n_cols % tn == 0
    return _matmul(kernel, [(x, kb, kd)], [(w, _wspec(w, lead, kb, kd, tn, col_start // tn), kd)], list(extra),
                   n_cols, out_dtype, tm=tm, tn=tn, name=name)


def _mm_out_major(kernel, x, wt, layer, row0, n_cols, out_dtype, *, tm, tn, extra=(), name):
    kd = x.shape[1]
    assert n_cols % tn == 0
    return _matmul(kernel, [(x, 0, kd)], [(wt, _wspec_out_major(layer, row0, kd, tn), kd)], list(extra),
                   n_cols, out_dtype, tm=tm, tn=tn, name=name)


def _pack_bf16_pairs(y):
    half = y.shape[1] // 2
    bits = lambda v: lax.bitcast_convert_type(v.astype(BF16).astype(F32), jnp.uint32)
    return (bits(y[:, :half]) >> 16) | bits(y[:, half:])


def _unpack_bf16_pairs(pk):
    lo = lax.bitcast_convert_type(pk << 16, F32).astype(BF16)
    hi = lax.bitcast_convert_type(pk & jnp.uint32(0xFFFF0000), F32).astype(BF16)
    return lo, hi


def _ln_kernel(h_ref, g_ref, b_ref, of_ref, ob_ref, *maybe_packed_ref):
    h = h_ref[...]
    mu = jnp.mean(h, axis=-1, keepdims=True)
    d = h - mu
    var = jnp.mean(d * d, axis=-1, keepdims=True)
    y = d * lax.rsqrt(var + LN_EPS) * g_ref[...] + b_ref[...]
    of_ref[...] = y
    ob_ref[...] = y.astype(BF16)
    for pk_ref in maybe_packed_ref:
        pk_ref[...] = _pack_bf16_pairs(y)


def _layer_norm(h, g, b, with_packed=False):
    rows, d = h.shape
    tr = _tile(rows, 256)
    row_spec = pl.BlockSpec((tr, d), lambda i: (i, 0))
    vec_spec = pl.BlockSpec((1, d), lambda i: (0, 0))
    out_shape = [jax.ShapeDtypeStruct((rows, d), F32), jax.ShapeDtypeStruct((rows, d), BF16)]
    out_specs = [row_spec, row_spec]
    if with_packed:
        out_shape.append(jax.ShapeDtypeStruct((rows, d // 2), jnp.uint32))
        out_specs.append(pl.BlockSpec((tr, d // 2), lambda i: (i, 0)))
    return pl.pallas_call(
        _ln_kernel,
        out_shape=tuple(out_shape),
        grid=(rows // tr,),
        in_specs=[row_spec, vec_spec, vec_spec],
        out_specs=tuple(out_specs),
        compiler_params=_params(1),
        name="layer_norm",
    )(h, g.reshape(1, d), b.reshape(1, d))


def _conv_branch_kernel(u_ref, up_ref, w_ref, g_ref, b_ref, o_ref, xin_ref, cv_ref, sh_ref, *, taps, sub_rows):
    rows, ch = u_ref.shape
    xin_ref[0:CONV_HALO, :] = jnp.where(pl.program_id(0) == 0, 0.0, up_ref[...])
    xin_ref[CONV_HALO:, :] = u_ref[...]

    lead = CONV_HALO - (taps - 1)
    phases = {}
    for k in range(taps):
        phases.setdefault((lead + k) % SUBLANES, []).append(k)
    shift_rows = sh_ref.shape[1]

    def chunk(c, carry):
        lanes = pl.ds(pl.multiple_of(c * LANES, LANES), LANES)
        for b in phases:
            if b > 0:
                sh_ref[b - 1] = xin_ref[pl.ds(b, shift_rows), lanes]
        for r in range(rows // sub_rows):
            acc = jnp.zeros((sub_rows, LANES), F32)
            for b, ks in phases.items():
                for k in ks:
                    start = r * sub_rows + (lead + k) // SUBLANES * SUBLANES
                    if b == 0:
                        window = xin_ref[pl.ds(start, sub_rows), lanes]
                    else:
                        window = sh_ref[b - 1, pl.ds(start, sub_rows), :]
                    acc = acc + window * w_ref[pl.ds(k, 1), lanes]
            cv_ref[pl.ds(r * sub_rows, sub_rows), lanes] = acc
        return carry

    lax.fori_loop(0, ch // LANES, chunk, 0)
    v = cv_ref[...]
    mu = jnp.mean(v, axis=-1, keepdims=True)
    d = v - mu
    var = jnp.mean(d * d, axis=-1, keepdims=True)
    y = d * lax.rsqrt(var + LN_EPS) * g_ref[...] + b_ref[...]
    o_ref[...] = _silu(y).astype(o_ref.dtype)


def _conv_branch(u, conv_w, ln_g, ln_b):
    rows, ch = u.shape
    taps = conv_w.shape[0]
    assert taps - 1 <= CONV_HALO
    tr = _tile(rows, 256)
    halo_blocks = tr // CONV_HALO
    kernel = functools.partial(_conv_branch_kernel, taps=taps, sub_rows=_tile(tr, 64))
    vec_spec = pl.BlockSpec((1, ch), lambda i: (0, 0))
    return pl.pallas_call(
        kernel,
        out_shape=jax.ShapeDtypeStruct((rows, ch), BF16),
        grid=(rows // tr,),
        in_specs=[pl.BlockSpec((tr, ch), lambda i: (i, 0)),
                  pl.BlockSpec((CONV_HALO, ch), lambda i: (jnp.maximum(i * halo_blocks - 1, 0), 0)),
                  pl.BlockSpec((taps, ch), lambda i: (0, 0)),
                  vec_spec, vec_spec],
        out_specs=pl.BlockSpec((tr, ch), lambda i: (i, 0)),
        scratch_shapes=[pltpu.VMEM((tr + CONV_HALO, ch), F32), pltpu.VMEM((tr, ch), F32),
                        pltpu.VMEM((SUBLANES - 1, tr + CONV_HALO - SUBLANES, LANES), F32)],
        compiler_params=_params(1),
        name="conv_branch",
    )(u, u, conv_w, ln_g.reshape(1, ch), ln_b.reshape(1, ch))


def _split3(v):
    hi = v.astype(BF16)
    r1 = v - hi.astype(F32)
    mid = r1.astype(BF16)
    lo = (r1 - mid.astype(F32)).astype(BF16)
    return hi, mid, lo


def _dt_kernel(raw_ref, bias_ref, alog_ref, tri_ref, place_ref,
               cs_ref, cst_ref, dt3_ref, ecs3_ref, dte3_ref, cdec3_ref, *, heads):
    q = raw_ref.shape[0]
    lane = lax.broadcasted_iota(jnp.int32, (q, LANES), 1)
    valid = lane < heads
    pre = raw_ref[...] + bias_ref[...]
    dt = jnp.where(valid, jnp.maximum(pre, 0.0) + jnp.log(1.0 + jnp.exp(-jnp.abs(pre))), 0.0)
    d_a = dt * (-jnp.exp(alog_ref[...]))
    tri = tri_ref[...]
    cs = sum(_dot(tri, piece) for piece in _split3(d_a))
    cs = jnp.where(valid, cs, 0.0)
    cs_end = cs[q - 1:q, :]
    cs_ref[...] = cs
    cst_ref[...] = cs.T

    def stack3(v):
        return sum(_dot(piece, place_ref[j]) for j, piece in enumerate(_split3(v))).astype(BF16)

    dt3_ref[...] = stack3(dt)
    ecs3_ref[...] = stack3(jnp.exp(cs))
    dte3_ref[...] = stack3(jnp.exp(cs_end - cs))
    cdec3_ref[...] = stack3(jnp.broadcast_to(jnp.exp(cs_end), (SUBLANES, LANES)))


def _dt_quantities(dt_raw, dt_bias, a_log, q):
    rows = dt_raw.shape[0]
    heads = dt_bias.shape[0]
    assert 3 * heads <= LANES
    n_chunks = rows // q
    pad = lambda v: jnp.pad(v.astype(F32), (0, LANES - heads)).reshape(1, LANES)
    tri = (lax.broadcasted_iota(jnp.int32, (q, q), 0) >= lax.broadcasted_iota(jnp.int32, (q, q), 1)).astype(BF16)
    src = lax.broadcasted_iota(jnp.int32, (3, LANES, LANES), 1)
    dst = lax.broadcasted_iota(jnp.int32, (3, LANES, LANES), 2)
    piece = lax.broadcasted_iota(jnp.int32, (3, LANES, LANES), 0)
    place = ((src < heads) & (dst == src + piece * heads)).astype(BF16)
    row_spec = pl.BlockSpec((q, LANES), lambda i: (i, 0))
    vec_spec = pl.BlockSpec((1, LANES), lambda i: (0, 0))
    return pl.pallas_call(
        functools.partial(_dt_kernel, heads=heads),
        out_shape=(jax.ShapeDtypeStruct((rows, LANES), F32),
                   jax.ShapeDtypeStruct((LANES, rows), F32),
                   jax.ShapeDtypeStruct((rows, LANES), BF16),
                   jax.ShapeDtypeStruct((rows, LANES), BF16),
                   jax.ShapeDtypeStruct((rows, LANES), BF16),
                   jax.ShapeDtypeStruct((n_chunks, SUBLANES, LANES), BF16)),
        grid=(n_chunks,),
        in_specs=[row_spec, vec_spec, vec_spec,
                  pl.BlockSpec((q, q), lambda i: (0, 0)),
                  pl.BlockSpec((3, LANES, LANES), lambda i: (0, 0, 0))],
        out_specs=(row_spec, pl.BlockSpec((LANES, q), lambda i: (0, i)), row_spec, row_spec, row_spec,
                   pl.BlockSpec((None, SUBLANES, LANES), lambda i: (i, 0, 0))),
        compiler_params=_params(1),
        name="ssd_dt",
    )(dt_raw, pad(dt_bias), pad(a_log), tri, place)


def _ssd_kernel(xbc_ref, halo_ref, z_ref, cs_ref, cst_ref, dt3_ref, ecs3_ref, dte3_ref, cdec3_ref,
                expand_ref, cw_ref, cb_ref, dskip_ref, gnorm_ref, o_ref, xin_ref, state_ref,
                *, d_inner, groups):
    q = xbc_ref.shape[0]
    gw = d_inner // groups
    n_state = (xbc_ref.shape[1] - d_inner) // (2 * groups)
    first = pl.program_id(0) == 0

    @pl.when(first)
    def _():
        state_ref[...] = jnp.zeros_like(state_ref)

    xin_ref[0:SSM_HALO, :] = jnp.where(first, 0.0, halo_ref[...])
    xin_ref[SSM_HALO:, :] = xbc_ref[...]
    taps = cw_ref.shape[0]
    acc = jnp.broadcast_to(cb_ref[...], xbc_ref.shape)
    for k in range(taps):
        acc = acc + xin_ref[pl.ds(SSM_HALO - (taps - 1) + k, q), :] * cw_ref[pl.ds(k, 1), :]
    xbc = _silu(acc)

    expand = expand_ref[...]
    dt_x = _dot(dt3_ref[...], expand)
    ecs_x = _dot(ecs3_ref[...], expand)
    dte_x = _dot(dte3_ref[...], expand)
    cdec_x = _dot(cdec3_ref[...], expand)[0:1, :]

    xs = xbc[:, :d_inner]
    xdt = xs * dt_x
    xdt_b = xdt.astype(BF16)
    xde_b = (xdt * dte_x).astype(BF16)
    cs = cs_ref[...]
    cst = cst_ref[...]
    causal = lax.broadcasted_iota(jnp.int32, (q, q), 0) >= lax.broadcasted_iota(jnp.int32, (q, q), 1)
    low_half = lax.broadcasted_iota(jnp.int32, (q, LANES), 1) < SSM_HEAD_DIM
    heads_per_pair = LANES // SSM_HEAD_DIM

    for g in range(groups):
        cols = slice(g * gw, (g + 1) * gw)
        b_g = xbc[:, d_inner + g * n_state:d_inner + (g + 1) * n_state].astype(BF16)
        c_g = xbc[:, d_inner + (groups + g) * n_state:d_inner + (groups + g + 1) * n_state].astype(BF16)
        scores = lax.dot_general(c_g, b_g, (((1,), (1,)), ((), ())), preferred_element_type=F32)
        state = state_ref[g]
        y_off = _dot(c_g, state.astype(BF16)) * ecs_x[:, cols]
        chunk_state = lax.dot_general(b_g, xde_b[:, cols], (((0,), (0,)), ((), ())), preferred_element_type=F32)
        state_ref[g] = state * cdec_x[:, cols] + chunk_state

        pair_out = []
        for p in range(gw // LANES):
            lanes = slice(g * gw + p * LANES, g * gw + (p + 1) * LANES)
            x_pair = xdt_b[:, lanes]
            halves = []
            for e in range(heads_per_pair):
                h = (g * gw + p * LANES) // SSM_HEAD_DIM + e
                seg = cs[:, h:h + 1] - cst[h:h + 1, :]
                decay = jnp.where(causal, jnp.exp(seg), 0.0)
                halves.append(_dot((scores * decay).astype(BF16), x_pair))
            pair_out.append(jnp.where(low_half, halves[0], halves[1]))
        y_diag = jnp.concatenate(pair_out, axis=1) if len(pair_out) > 1 else pair_out[0]

        y = y_diag + y_off + dskip_ref[:, cols] * xs[:, cols]
        gated = y * _silu(z_ref[:, cols])
        inv = lax.rsqrt(jnp.mean(gated * gated, axis=-1, keepdims=True) + LN_EPS)
        o_ref[:, cols] = (gated * inv * gnorm_ref[:, cols]).astype(o_ref.dtype)


def _ssd(xbc_raw, z, dt_parts, conv_w, conv_b, d_skip, norm_g, q):
    rows, xbc_dim = xbc_raw.shape
    d_inner = z.shape[1]
    heads = d_skip.shape[0]
    n_state = (xbc_dim - d_inner) // (2 * SSM_GROUPS)
    gw = d_inner // SSM_GROUPS
    assert heads * SSM_HEAD_DIM == d_inner and gw % LANES == 0 and n_state == D_STATE
    cs, cst, dt3, ecs3, dte3, cdec3 = dt_parts
    j = lax.broadcasted_iota(jnp.int32, (LANES, d_inner), 0)
    c = lax.broadcasted_iota(jnp.int32, (LANES, d_inner), 1)
    expand = ((j < 3 * heads) & (j % heads == c // SSM_HEAD_DIM)).astype(BF16)
    dskip_x = jnp.repeat(d_skip.astype(F32), SSM_HEAD_DIM).reshape(1, d_inner)
    row = lambda w: pl.BlockSpec((q, w), lambda i: (i, 0))
    const = lambda s: pl.BlockSpec(s, lambda i: (0,) * len(s))
    halo_blocks = q // SSM_HALO
    kernel = functools.partial(_ssd_kernel, d_inner=d_inner, groups=SSM_GROUPS)
    return pl.pallas_call(
        kernel,
        out_shape=jax.ShapeDtypeStruct((rows, d_inner), BF16),
        grid=(rows // q,),
        in_specs=[row(xbc_dim),
                  pl.BlockSpec((SSM_HALO, xbc_dim), lambda i: (jnp.maximum(i * halo_blocks - 1, 0), 0)),
                  row(d_inner), row(LANES),
                  pl.BlockSpec((LANES, q), lambda i: (0, i)),
                  row(LANES), row(LANES), row(LANES),
                  pl.BlockSpec((None, SUBLANES, LANES), lambda i: (i, 0, 0)),
                  const((LANES, d_inner)), const((SSM_CONV_K, xbc_dim)), const((1, xbc_dim)),
                  const((1, d_inner)), const((1, d_inner))],
        out_specs=row(d_inner),
        scratch_shapes=[pltpu.VMEM((q + SSM_HALO, xbc_dim), F32),
                        pltpu.VMEM((SSM_GROUPS, n_state, gw), F32)],
        compiler_params=_params(1),
        name="ssd_scan",
    )(xbc_raw, xbc_raw, z, cs, cst, dt3, ecs3, dte3, cdec3, expand, conv_w,
      conv_b.reshape(1, xbc_dim), dskip_x, norm_g.reshape(1, d_inner))


MOE_ROW_TILE = 512
MOE_TOKEN_TILE = 256
DMA_LOOP_UNROLL = 8


def _router_kernel(x_ref, w_ref, tri_ref, wgt_ref, sel_ref, cnt_ref, carry_ref, *, n_experts):
    @pl.when(pl.program_id(0) == 0)
    def _():
        carry_ref[...] = jnp.zeros_like(carry_ref)

    logits = _dot(x_ref[...], w_ref[...].astype(BF16))
    lane_i = lax.broadcasted_iota(jnp.int32, logits.shape, 1)
    lane = lane_i.astype(F32)
    neg = -jnp.inf
    lg = jnp.where(lane_i < n_experts, logits, neg)
    m1 = jnp.max(lg, axis=-1, keepdims=True)
    i1 = jnp.min(jnp.where(lg == m1, lane, float(LANES)), axis=-1, keepdims=True)
    lg2 = jnp.where(lane == i1, neg, lg)
    m2 = jnp.max(lg2, axis=-1, keepdims=True)
    i2 = jnp.min(jnp.where(lg2 == m2, lane, float(LANES)), axis=-1, keepdims=True)
    e2 = jnp.exp(m2 - m1)
    w1 = 1.0 / (1.0 + e2)
    wgt_ref[...] = jnp.where(lane_i == 0, w1, jnp.where(lane_i == 1, e2 * w1, 0.0))

    sel1 = lane == i1
    sel2 = lane == i2
    onehot = jnp.where(sel1, 1.0, jnp.where(sel2, 1.0, 0.0))
    earlier = _dot(tri_ref[...], onehot.astype(BF16)) + carry_ref[0:1, :]
    rank1 = jnp.sum(jnp.where(sel1, earlier, 0.0), axis=-1, keepdims=True)
    rank2 = jnp.sum(jnp.where(sel2, earlier, 0.0), axis=-1, keepdims=True)
    sel = jnp.where(lane_i == 0, i1, jnp.where(lane_i == 1, i2,
                    jnp.where(lane_i == 2, rank1, jnp.where(lane_i == 3, rank2, 0.0))))
    sel_ref[...] = sel.astype(jnp.int32)
    carry_ref[...] = carry_ref[...] + jnp.sum(onehot, axis=0, keepdims=True)
    cnt_ref[...] = carry_ref[...].astype(jnp.int32)


def _router(x_bf, w_router):
    rows, d = x_bf.shape
    n_experts = w_router.shape[1]
    w_pad = jnp.pad(w_router, ((0, 0), (0, LANES - n_experts)))
    tr = _tile(rows, 512)
    tri = (lax.broadcasted_iota(jnp.int32, (tr, tr), 0) > lax.broadcasted_iota(jnp.int32, (tr, tr), 1)).astype(BF16)
    row_spec = pl.BlockSpec((tr, LANES), lambda i: (i, 0))
    return pl.pallas_call(
        functools.partial(_router_kernel, n_experts=n_experts),
        out_shape=(jax.ShapeDtypeStruct((rows, LANES), F32), jax.ShapeDtypeStruct((rows, LANES), jnp.int32),
                   jax.ShapeDtypeStruct((SUBLANES, LANES), jnp.int32)),
        grid=(rows // tr,),
        in_specs=[pl.BlockSpec((tr, d), lambda i: (i, 0)), pl.BlockSpec((d, LANES), lambda i: (0, 0)),
                  pl.BlockSpec((tr, tr), lambda i: (0, 0))],
        out_specs=(row_spec, row_spec, pl.BlockSpec((SUBLANES, LANES), lambda i: (0, 0))),
        scratch_shapes=[pltpu.VMEM((SUBLANES, LANES), F32)],
        compiler_params=_params(1),
        name="moe_router",
    )(x_bf, w_pad, tri)


def _row_copy(src, src_row, dst, dst_row, sem):
    return pltpu.make_async_copy(src.at[pl.ds(src_row, 1), :], dst.at[pl.ds(dst_row, 1), :], sem)


def _gather_kernel(pos1_ref, pos2_ref, filled_ref, x_hbm, o_ref, token_of_ref, sem_ref):
    tile_rows = o_ref.shape[0]
    step = pl.program_id(0)

    @pl.when(step == 0)
    def _():
        def record(t, carry):
            token_of_ref[pos1_ref[t]] = t
            token_of_ref[pos2_ref[t]] = t
            return carry

        lax.fori_loop(0, pos1_ref.shape[0], record, 0, unroll=DMA_LOOP_UNROLL)

    filled = filled_ref[step]
    base = step * tile_rows

    def start(r, carry):
        _row_copy(x_hbm, token_of_ref[base + r], o_ref, r, sem_ref).start()
        return carry

    def wait(r, carry):
        _row_copy(x_hbm, 0, o_ref, 0, sem_ref).wait()
        return carry

    @pl.when(filled == tile_rows)
    def _():
        lax.fori_loop(0, tile_rows, start, 0, unroll=DMA_LOOP_UNROLL)
        lax.fori_loop(0, tile_rows, wait, 0, unroll=DMA_LOOP_UNROLL)

    @pl.when(filled < tile_rows)
    def _():
        o_ref[...] = jnp.zeros_like(o_ref)
        lax.fori_loop(0, filled, start, 0)
        lax.fori_loop(0, filled, wait, 0)


def _moe_gather(x_pk, pos1, pos2, filled):
    rows, w = x_pk.shape
    n_tiles = filled.shape[0]
    return pl.pallas_call(
        _gather_kernel,
        out_shape=jax.ShapeDtypeStruct((n_tiles * MOE_ROW_TILE, w), x_pk.dtype),
        grid_spec=pltpu.PrefetchScalarGridSpec(
            num_scalar_prefetch=3, grid=(n_tiles,),
            in_specs=[pl.BlockSpec(memory_space=pl.ANY)],
            out_specs=pl.BlockSpec((MOE_ROW_TILE, w), lambda i, p1, p2, fl: (i, 0)),
            scratch_shapes=[pltpu.SMEM((n_tiles * MOE_ROW_TILE,), jnp.int32), pltpu.SemaphoreType.DMA(())]),
        compiler_params=_params(1),
        name="moe_gather",
    )(pos1, pos2, filled, x_pk)


def _moe_up_kernel(e_ref, c_ref, rt_ref, first_ref, live_ref, x_ref, wg_ref, wu_ref, o_ref, wgb_ref, wub_ref):
    half = x_ref.shape[1]
    step = pl.program_id(0)

    @pl.when(first_ref[step] == 1)
    def _():
        wgb_ref[...] = wg_ref[...].astype(BF16)
        wub_ref[...] = wu_ref[...].astype(BF16)

    @pl.when(live_ref[step] == 1)
    def _():
        lo, hi = _unpack_bf16_pairs(x_ref[...])
        gate = _dot(lo, wgb_ref[0:half, :]) + _dot(hi, wgb_ref[half:, :])
        up = _dot(lo, wub_ref[0:half, :]) + _dot(hi, wub_ref[half:, :])
        o_ref[...] = (_silu(gate) * up).astype(o_ref.dtype)

    @pl.when(live_ref[step] == 0)
    def _():
        o_ref[...] = jnp.zeros_like(o_ref)


def _moe_down_kernel(e_ref, c_ref, rt_ref, first_ref, live_ref, h_ref, w_ref, o_ref, wb_ref):
    step = pl.program_id(0)

    @pl.when(first_ref[step] == 1)
    def _():
        wb_ref[...] = w_ref[...].astype(BF16)

    @pl.when(live_ref[step] == 1)
    def _():
        o_ref[...] = _dot(h_ref[...], wb_ref[...])

    @pl.when(live_ref[step] == 0)
    def _():
        o_ref[...] = jnp.zeros_like(o_ref)


def _group_layout(counts, n_tiles):
    n_experts = counts.shape[0]
    tiles = (counts + MOE_ROW_TILE - 1) // MOE_ROW_TILE
    last = jnp.max(jnp.where(tiles > 0, jnp.arange(n_experts), 0))
    tiles = tiles.at[last].add(n_tiles - jnp.sum(tiles))
    tile_end = jnp.cumsum(tiles)
    starts = (tile_end - tiles) * MOE_ROW_TILE
    tile = jnp.arange(n_tiles)
    e = jnp.sum((tile[:, None] >= tile_end[None, :]).astype(jnp.int32), axis=1)
    local = tile - (tile_end - tiles)[e]
    filled = jnp.clip(counts[e] - local * MOE_ROW_TILE, 0, MOE_ROW_TILE)
    return tiles.astype(jnp.int32), starts.astype(jnp.int32), filled.astype(jnp.int32)


def _work_list(tiles, counts, n_col_tiles, n_tiles):
    steps_end = jnp.cumsum(tiles * n_col_tiles)
    i = jnp.arange(n_tiles * n_col_tiles, dtype=jnp.int32)
    e = jnp.sum((i[:, None] >= steps_end[None, :]).astype(jnp.int32), axis=1)
    k = i - (steps_end[e] - tiles[e] * n_col_tiles)
    col = k // tiles[e]
    local = k % tiles[e]
    row_tile = (jnp.cumsum(tiles) - tiles)[e] + local
    live = local * MOE_ROW_TILE < counts[e]
    return (e, col.astype(jnp.int32), row_tile.astype(jnp.int32), (local == 0).astype(jnp.int32),
            live.astype(jnp.int32))


def _moe_experts(xg, tiles, counts, li, p):
    rows_g = xg.shape[0]
    d, f = p["moe_w_gate"].shape[2:]
    n_tiles = rows_g // MOE_ROW_TILE

    tf = _tile(f, 512)
    up_list = _work_list(tiles, counts, f // tf, n_tiles)
    w_up_spec = lambda: pl.BlockSpec((None, None, d, tf), lambda i, e, c, rt, fi, lv: (li, e[i], 0, c[i]))
    hg = pl.pallas_call(
        _moe_up_kernel,
        out_shape=jax.ShapeDtypeStruct((rows_g, f), BF16),
        grid_spec=pltpu.PrefetchScalarGridSpec(
            num_scalar_prefetch=5, grid=(n_tiles * (f // tf),),
            in_specs=[pl.BlockSpec((MOE_ROW_TILE, d // 2), lambda i, e, c, rt, fi, lv: (rt[i], 0)),
                      w_up_spec(), w_up_spec()],
            out_specs=pl.BlockSpec((MOE_ROW_TILE, tf), lambda i, e, c, rt, fi, lv: (rt[i], c[i])),
            scratch_shapes=[pltpu.VMEM((d, tf), BF16), pltpu.VMEM((d, tf), BF16)]),
        compiler_params=_params(1),
        name="moe_up",
    )(*up_list, xg, p["moe_w_gate"], p["moe_w_up"])

    tn = _tile(d, 2048)
    down_list = _work_list(tiles, counts, d // tn, n_tiles)
    return pl.pallas_call(
        _moe_down_kernel,
        out_shape=jax.ShapeDtypeStruct((rows_g, d), F32),
        grid_spec=pltpu.PrefetchScalarGridSpec(
            num_scalar_prefetch=5, grid=(n_tiles * (d // tn),),
            in_specs=[pl.BlockSpec((MOE_ROW_TILE, f), lambda i, e, c, rt, fi, lv: (rt[i], 0)),
                      pl.BlockSpec((None, None, f, tn), lambda i, e, c, rt, fi, lv: (li, e[i], 0, c[i]))],
            out_specs=pl.BlockSpec((MOE_ROW_TILE, tn), lambda i, e, c, rt, fi, lv: (rt[i], c[i])),
            scratch_shapes=[pltpu.VMEM((f, tn), BF16)]),
        compiler_params=_params(1),
        name="moe_down",
    )(*down_list, hg, p["moe_w_down"])


def _combine_ln_kernel(pos1_ref, pos2_ref, x_ref, wgt_ref, yg_hbm, g_ref, b_ref, of_ref, ob_ref,
                       y1_ref, y2_ref, sem_ref, *, alpha):
    tokens = x_ref.shape[0]
    base = pl.program_id(0) * tokens

    def start(t, carry):
        _row_copy(yg_hbm, pos1_ref[base + t], y1_ref, t, sem_ref).start()
        _row_copy(yg_hbm, pos2_ref[base + t], y2_ref, t, sem_ref).start()
        return carry

    def wait(t, carry):
        _row_copy(yg_hbm, 0, y1_ref, 0, sem_ref).wait()
        _row_copy(yg_hbm, 0, y2_ref, 0, sem_ref).wait()
        return carry

    lax.fori_loop(0, tokens, start, 0, unroll=DMA_LOOP_UNROLL)
    lax.fori_loop(0, tokens, wait, 0, unroll=DMA_LOOP_UNROLL)
    wgt = wgt_ref[...]
    h = alpha * x_ref[...] + wgt[:, 0:1] * y1_ref[...] + wgt[:, 1:2] * y2_ref[...]
    mu = jnp.mean(h, axis=-1, keepdims=True)
    dlt = h - mu
    var = jnp.mean(dlt * dlt, axis=-1, keepdims=True)
    y = dlt * lax.rsqrt(var + LN_EPS) * g_ref[...] + b_ref[...]
    of_ref[...] = y
    ob_ref[...] = y.astype(BF16)


def _moe_combine_ln(x_f, wgt, pos1, pos2, yg, g, b, alpha):
    rows, d = x_f.shape
    tokens = _tile(rows, MOE_TOKEN_TILE)
    row_spec = pl.BlockSpec((tokens, d), lambda i, p1, p2: (i, 0))
    vec_spec = pl.BlockSpec((1, d), lambda i, p1, p2: (0, 0))
    return pl.pallas_call(
        functools.partial(_combine_ln_kernel, alpha=alpha),
        out_shape=(jax.ShapeDtypeStruct((rows, d), F32), jax.ShapeDtypeStruct((rows, d), BF16)),
        grid_spec=pltpu.PrefetchScalarGridSpec(
            num_scalar_prefetch=2, grid=(rows // tokens,),
            in_specs=[row_spec, pl.BlockSpec((tokens, LANES), lambda i, p1, p2: (i, 0)),
                      pl.BlockSpec(memory_space=pl.ANY), vec_spec, vec_spec],
            out_specs=(row_spec, row_spec),
            scratch_shapes=[pltpu.VMEM((tokens, d), F32), pltpu.VMEM((tokens, d), F32), pltpu.SemaphoreType.DMA(())]),
        compiler_params=_params(1),
        name="moe_combine_ln",
    )(pos1, pos2, x_f, wgt, yg, g.reshape(1, d), b.reshape(1, d))


def _mixer(x_f, x_b, layer, p, alpha, with_packed):
    rows, d = x_f.shape
    conv_ch = p["conv_dw"].shape[2]
    d_inner = p["ssm_norm_g"].shape[1]
    xbc_dim = p["ssm_conv_w"].shape[2]
    heads = p["dt_bias"].shape[1]
    w_in_t = p["w_in_t"]
    lead = (layer,)
    tm = _tile(rows, 1024)
    c_z = 2 * conv_ch
    c_xbc = c_z + d_inner
    c_dt = c_xbc + xbc_dim
    c_gate = c_dt + heads

    tn2 = _tile(conv_ch, 256)
    u = _matmul(_mm_glu_kernel, [(x_b, 0, d)],
                [(w_in_t, _wspec_out_major(layer, 0, d, tn2), d),
                 (w_in_t, _wspec_out_major(layer, conv_ch, d, tn2), d)], [], conv_ch, F32,
                tm=tm, tn=tn2, name="in_proj_glu")
    z = _mm_out_major(_mm_plain_kernel, x_b, w_in_t, layer, c_z, d_inner, F32, tm=tm, tn=_tile(d_inner, 512),
                      name="in_proj_z")
    xbc_raw = _mm_out_major(_mm_plain_kernel, x_b, w_in_t, layer, c_xbc, xbc_dim, F32, tm=tm,
                            tn=_tile(xbc_dim, 512), name="in_proj_xbc")
    dt_raw = _mm_out_major(_mm_plain_kernel, x_b, w_in_t, layer, c_dt, LANES, F32, tm=tm, tn=LANES,
                           name="in_proj_dt")
    tng = _tile(2 * d, 512)
    gates = _mm_out_major(_mm_sigmoid_bias_kernel, x_b, w_in_t, layer, c_gate, 2 * d, F32, tm=tm, tn=tng,
                          extra=[(p["b_gate"][layer].reshape(1, 2 * d), pl.BlockSpec((1, tng), lambda n, m: (0, n)))],
                          name="in_proj_gates")

    u_act = _conv_branch(u, p["conv_dw"][layer], p["conv_ln_g"][layer], p["conv_ln_b"][layer])

    q = _tile(rows, 256)
    dt_parts = _dt_quantities(dt_raw, p["dt_bias"][layer], p["a_log"][layer], q)
    y_act = _ssd(xbc_raw, z, dt_parts, p["ssm_conv_w"][layer], p["ssm_conv_b"][layer],
                 p["d_skip"][layer], p["ssm_norm_g"][layer], q)

    tnm = _tile(d, 512)
    gate_blocks = d // tnm
    merged = _matmul(
        _mm_merge_kernel, [(u_act, 0, conv_ch), (y_act, 0, d_inner)],
        [(p["conv_pw2"], _wspec(p["conv_pw2"], lead, 0, conv_ch, tnm, 0), conv_ch),
         (p["ssm_out"], _wspec(p["ssm_out"], lead, 0, d_inner, tnm, 0), d_inner)],
        [(gates, pl.BlockSpec((tm, tnm), lambda n, m: (m, n))),
         (gates, pl.BlockSpec((tm, tnm), lambda n, m: (m, n + gate_blocks)))],
        d, BF16, tm=tm, tn=tnm, name="branch_merge")

    h = _mm_single(functools.partial(_mm_residual_kernel, alpha=alpha), merged, p["w_o"], lead, 0, d, F32,
                   tm=tm, tn=tnm, extra=[(x_f, pl.BlockSpec((tm, tnm), lambda n, m: (m, n)))], name="out_proj")
    return _layer_norm(h, p["ln_mix_g"][layer], p["ln_mix_b"][layer], with_packed)


DOWN_PROJ_MAX_K = 6144


def _down_proj(h_act, w, lead, x_f, alpha, name):
    rows, f = h_act.shape
    d = x_f.shape[1]
    splits = pl.cdiv(f, DOWN_PROJ_MAX_K)
    assert f % splits == 0
    kd = f // splits
    tm = _tile(rows, 512)
    tn = _tile(d, 512)
    res_spec = pl.BlockSpec((tm, tn), lambda n, m: (m, n))
    acc, scale = x_f, alpha
    for kb in range(splits):
        acc = _mm_single(functools.partial(_mm_residual_kernel, alpha=scale), h_act, w, lead, 0, d, F32,
                         tm=tm, tn=tn, extra=[(acc, res_spec)], name=f"{name}_{kb}", kb=kb, kd=kd)
        scale = 1.0
    return acc


def _dense_ffn(x_f, x_b, i, p, alpha):
    rows, d = x_f.shape
    f = p["ffn_w_gate"].shape[2]
    tm = _tile(rows, 1024)
    tn = _tile(f, 256)
    h_act = _matmul(_mm_swiglu_kernel, [(x_b, 0, d)],
                    [(p["ffn_w_gate"], _wspec(p["ffn_w_gate"], (i,), 0, d, tn, 0), d),
                     (p["ffn_w_up"], _wspec(p["ffn_w_up"], (i,), 0, d, tn, 0), d)], [], f, BF16,
                    tm=tm, tn=tn, name="ffn_up")
    return _down_proj(h_act, p["ffn_w_down"], (i,), x_f, alpha, "ffn_down")


def _moe_ffn_ln(x_f, x_b, x_pk, i, layer, p, alpha):
    rows = x_f.shape[0]
    n_experts = p["moe_w_gate"].shape[1]
    n_tiles = pl.cdiv(TOP_K * rows, MOE_ROW_TILE) + n_experts
    wgt, sel, counts = _router(x_b, p["moe_router"][i])
    counts = counts[0, :n_experts]
    tiles, starts, filled = _group_layout(counts, n_tiles)
    pos1 = starts[sel[:, 0]] + sel[:, 2]
    pos2 = starts[sel[:, 1]] + sel[:, 3]
    xg = _moe_gather(x_pk, pos1, pos2, filled)
    yg = _moe_experts(xg, tiles, counts, i, p)
    return _moe_combine_ln(x_f, wgt, pos1, pos2, yg, p["ln_ffn_g"][layer], p["ln_ffn_b"][layer], alpha)


def kernel(x, w_in, b_gate, conv_dw, conv_ln_g, conv_ln_b, conv_pw2, ssm_conv_w, ssm_conv_b, dt_bias, a_log,
           d_skip, ssm_norm_g, ssm_out, w_o, ln_mix_g, ln_mix_b, ffn_w_gate, ffn_w_up, ffn_w_down, moe_router,
           moe_w_gate, moe_w_up, moe_w_down, ln_ffn_g, ln_ffn_b):
    p = dict(w_in_t=jnp.swapaxes(w_in, 1, 2), b_gate=b_gate, conv_dw=conv_dw, conv_ln_g=conv_ln_g, conv_ln_b=conv_ln_b,
             conv_pw2=conv_pw2, ssm_conv_w=ssm_conv_w, ssm_conv_b=ssm_conv_b, dt_bias=dt_bias, a_log=a_log,
             d_skip=d_skip, ssm_norm_g=ssm_norm_g, ssm_out=ssm_out, w_o=w_o, ln_mix_g=ln_mix_g,
             ln_mix_b=ln_mix_b, ffn_w_gate=ffn_w_gate, ffn_w_up=ffn_w_up, ffn_w_down=ffn_w_down,
             moe_router=moe_router, moe_w_gate=moe_w_gate, moe_w_up=moe_w_up, moe_w_down=moe_w_down,
             ln_ffn_g=ln_ffn_g, ln_ffn_b=ln_ffn_b)
    batch = x.shape[0]
    depth = w_in.shape[0]
    alpha = (2.0 * depth) ** 0.25
    outs = []
    for b in range(batch):
        x_f = x[b]
        x_b = x_f.astype(BF16)
        for layer in range(depth):
            i = layer // 2
            moe = layer % 2 == 1
            x_f, x_b, *x_pk = _mixer(x_f, x_b, layer, p, alpha, with_packed=moe)
            if moe:
                x_f, x_b = _moe_ffn_ln(x_f, x_b, x_pk[0], i, layer, p, alpha)
            else:
                h = _dense_ffn(x_f, x_b, i, p, alpha)
                x_f, x_b = _layer_norm(h, ln_ffn_g[layer], ln_ffn_b[layer])
        outs.append(x_f)
    return jnp.stack(outs, axis=0)
```
